```python
import jax, jax.numpy as jnp
from jax import lax
import numpy as np

D_MODEL = 1024
BATCH = 8
SEQ = 4096
DEPTH = 4

N_META = 16
MIX_WIDTH = 2 * D_MODEL
DN_HEAD_DIM = 128
DN_WIDTH = D_MODEL
DN_HEADS = DN_WIDTH // DN_HEAD_DIM
DN_CONV = 4
CHUNK = 64
SC_WIDTH = MIX_WIDTH - DN_WIDTH
SC_CONV = 3
EPS = 1e-6

SPLITS = (
    3 * DN_WIDTH,
    4 * DN_WIDTH,
    4 * DN_WIDTH + DN_HEADS,
    4 * DN_WIDTH + 2 * DN_HEADS,
    4 * DN_WIDTH + 2 * DN_HEADS + SC_WIDTH,
    4 * DN_WIDTH + 2 * DN_HEADS + 2 * SC_WIDTH,
    4 * DN_WIDTH + 2 * DN_HEADS + 3 * SC_WIDTH,
)
IN_COLS = 4 * DN_WIDTH + 2 * DN_HEADS + 4 * SC_WIDTH

kernel_name = "hymba_gdn_shortconv_hybrid"


def rmsnorm(x, gain):
    xf = x.astype(jnp.float32)
    y = xf * lax.rsqrt(jnp.mean(xf * xf, axis=-1, keepdims=True) + EPS)
    return (y * gain.astype(jnp.float32)).astype(x.dtype)


def l2norm(x):
    return x * lax.rsqrt(jnp.sum(x * x, axis=-1, keepdims=True) + EPS)


def causal_dwconv(x, w):
    K = w.shape[0]
    L = x.shape[1]
    xp = jnp.pad(x, ((0, 0), (K - 1, 0), (0, 0)))
    y = xp[:, 0:L, :] * w[0]
    for j in range(1, K):
        y = y + xp[:, j:j + L, :] * w[j]
    return y


def chunk_gated_delta_rule(q, k, v, g, beta):
    Bsz, L, H, dk = q.shape
    dv = v.shape[-1]
    N = L // CHUNK

    def to_chunks(t):
        t = t.reshape((Bsz, N, CHUNK, H) + t.shape[3:])
        return jnp.moveaxis(t, 3, 1)

    q, k, v, g, beta = (to_chunks(t) for t in (q, k, v, g, beta))
    g = jnp.cumsum(g, axis=-1)
    idx = jnp.arange(CHUNK)
    causal = idx[:, None] >= idx[None, :]
    strict = idx[:, None] > idx[None, :]
    decay = jnp.exp(jnp.where(causal, g[..., :, None] - g[..., None, :], -jnp.inf))
    kk = jnp.einsum('bhnid,bhnjd->bhnij', k, k)
    a_low = jnp.where(strict, beta[..., :, None] * kk * decay, 0.0)
    tmat = a_low + jnp.eye(CHUNK, dtype=a_low.dtype)
    u = lax.linalg.triangular_solve(tmat, v * beta[..., None], left_side=True, lower=True, unit_diagonal=True)
    w = lax.linalg.triangular_solve(tmat, k * (beta * jnp.exp(g))[..., None], left_side=True, lower=True, unit_diagonal=True)
    qk = jnp.einsum('bhnid,bhnjd->bhnij', q, k) * decay
    q_dec = q * jnp.exp(g)[..., None]
    g_last = g[..., -1]
    k_dec = k * jnp.exp(g_last[..., None] - g)[..., None]
    xs = tuple(jnp.moveaxis(t, 2, 0) for t in (u, w, qk, q_dec, k_dec, jnp.exp(g_last)))

    def step(S, inp):
        u_c, w_c, qk_c, qd_c, kd_c, gl_c = inp
        v_new = u_c - jnp.einsum('bhcd,bhde->bhce', w_c, S)
        o = jnp.einsum('bhcd,bhde->bhce', qd_c, S) + jnp.einsum('bhij,bhje->bhie', qk_c, v_new)
        S = S * gl_c[..., None, None] + jnp.einsum('bhcd,bhce->bhde', kd_c, v_new)
        return S, o

    S0 = jnp.zeros((Bsz, H, dk, dv), jnp.float32)
    _, o = lax.scan(step, S0, xs)
    o = jnp.moveaxis(o, 0, 2)
    return jnp.moveaxis(o, 1, 3).reshape(Bsz, L, H, dv)


def hybrid_layer(x, norm_g, w_in, dn_conv_w, dn_A_log, dn_dt_bias, dn_out_g, sc_conv_w, w_out):
    Bsz, L, _ = x.shape
    f32 = jnp.float32
    h = rmsnorm(x, norm_g)
    p = h @ w_in
    qkv, dn_z, dn_a, dn_b, sc_b, sc_c, sc_h, sc_z = jnp.split(p, SPLITS, axis=-1)

    qkv = jax.nn.silu(causal_dwconv(qkv, dn_conv_w)).astype(f32)
    q, k, v = jnp.split(qkv, 3, axis=-1)
    q = l2norm(q.reshape(Bsz, L, DN_HEADS, DN_HEAD_DIM)) * (DN_HEAD_DIM ** -0.5)
    k = l2norm(k.reshape(Bsz, L, DN_HEADS, DN_HEAD_DIM))
    v = v.reshape(Bsz, L, DN_HEADS, DN_HEAD_DIM)
    g = -jnp.exp(dn_A_log.astype(f32)) * jax.nn.softplus(dn_a.astype(f32) + dn_dt_bias.astype(f32))
    beta = jax.nn.sigmoid(dn_b.astype(f32))
    pad_front = (-N_META) % CHUNK
    pad_back = (-(pad_front + L)) % CHUNK

    def padt(t):
        return jnp.pad(t, ((0, 0), (pad_front, pad_back)) + ((0, 0),) * (t.ndim - 2))

    o = chunk_gated_delta_rule(padt(q), padt(k), padt(v), padt(g), padt(beta))[:, pad_front:pad_front + L]
    o = rmsnorm(o, dn_out_g) * jax.nn.silu(dn_z.astype(f32).reshape(Bsz, L, DN_HEADS, DN_HEAD_DIM))
    o_dn = o.reshape(Bsz, L, DN_WIDTH).astype(x.dtype)

    y = sc_b * causal_dwconv(sc_c * sc_h, sc_conv_w)
    o_sc = (y * jax.nn.silu(sc_z)).astype(x.dtype)

    mix = jnp.concatenate([o_dn, o_sc], axis=-1)
    return x + mix @ w_out


def setup_inputs(seed: int = 0) -> dict:
    key = jax.random.key(seed)
    ks = jax.random.split(key, 12)
    x = jax.random.normal(ks[0], (BATCH, SEQ, D_MODEL), jnp.float32)
    meta_tokens = jax.random.normal(ks[1], (N_META, D_MODEL), jnp.float32)
    norm_g = 1.0 + 0.02 * jax.random.normal(ks[2], (DEPTH, D_MODEL), jnp.float32)
    w_in = jax.random.normal(ks[3], (DEPTH, D_MODEL, IN_COLS), jnp.float32) * (D_MODEL ** -0.5)
    dn_conv_w = jax.random.normal(ks[4], (DEPTH, DN_CONV, 3 * DN_WIDTH), jnp.float32) * (DN_CONV ** -0.5)
    dn_A_log = jnp.log(jax.random.uniform(ks[5], (DEPTH, DN_HEADS), jnp.float32, 1.0, 16.0))
    dt = jnp.exp(jax.random.uniform(ks[6], (DEPTH, DN_HEADS), jnp.float32, np.log(1e-3), np.log(1e-1)))
    dn_dt_bias = dt + jnp.log(-jnp.expm1(-dt))
    dn_out_g = 1.0 + 0.02 * jax.random.normal(ks[7], (DEPTH, DN_HEAD_DIM), jnp.float32)
    sc_conv_w = jax.random.normal(ks[8], (DEPTH, SC_CONV, SC_WIDTH), jnp.float32) * (SC_CONV ** -0.5)
    w_out = jax.random.normal(ks[9], (DEPTH, MIX_WIDTH, D_MODEL), jnp.float32) * (MIX_WIDTH ** -0.5)
    final_g = 1.0 + 0.02 * jax.random.normal(ks[10], (D_MODEL,), jnp.float32)
    return {"x": x, "meta_tokens": meta_tokens, "norm_g": norm_g, "w_in": w_in,
            "dn_conv_w": dn_conv_w, "dn_A_log": dn_A_log, "dn_dt_bias": dn_dt_bias,
            "dn_out_g": dn_out_g, "sc_conv_w": sc_conv_w, "w_out": w_out, "final_g": final_g}


def reference(x, meta_tokens, norm_g, w_in, dn_conv_w, dn_A_log, dn_dt_bias, dn_out_g, sc_conv_w, w_out, final_g):
    Bsz = x.shape[0]
    meta = jnp.broadcast_to(meta_tokens[None].astype(x.dtype), (Bsz, N_META, D_MODEL))
    h = jnp.concatenate([meta, x], axis=1)
    for l in range(DEPTH):
        h = hybrid_layer(h, norm_g[l], w_in[l], dn_conv_w[l], dn_A_log[l], dn_dt_bias[l],
                         dn_out_g[l], sc_conv_w[l], w_out[l])
    return rmsnorm(h[:, N_META:], final_g)
```

```python
import functools

import jax
import jax.numpy as jnp
from jax import lax
from jax.experimental import pallas as pl
from jax.experimental.pallas import tpu as pltpu

F32 = jnp.float32
BF16 = jnp.bfloat16

D_MODEL = 1024
N_META = 16
HEADS = 8
HEAD_DIM = 128
DN_CONV = 4
SC_CONV = 3
CHUNK = 64
EPS = 1e-6
PAD_FRONT = (-N_META) % CHUNK
LANES = 128
SUBLANES = 8
ROW_BLOCK = 320
VMEM_LIMIT = 56 * 1024 * 1024


def _silu(x):
    return x * (1.0 / (1.0 + jnp.exp(-x)))


def _softplus(x):
    return jnp.maximum(x, 0.0) + jnp.log(1.0 + jnp.exp(-jnp.abs(x)))


def _const_spec(shape):
    nd = len(shape)
    return pl.BlockSpec(shape, lambda *_: (0,) * nd, pipeline_mode=pl.Buffered(1))


def _row_spec(rows, cols):
    return pl.BlockSpec((None, rows, cols), lambda b, j: (b, j, 0))


def _inproj_body(x_ref, ng_ref, wqkv_ref, wz_ref, wab_ref, wb_ref, wc_ref, wh_ref, wzb_ref,
                 cw_ref, scw_ref, alog_ref, dtb_ref,
                 q_ref, k_ref, v_ref, sz_ref, osc_ref, gb_ref,
                 qkv_scr, sc_scr, *, tb):
    j = pl.program_id(1)

    @pl.when(j == 0)
    def _():
        qkv_scr[0:SUBLANES, :] = jnp.zeros((SUBLANES, 3 * D_MODEL), F32)
        sc_scr[0:SUBLANES, :] = jnp.zeros((SUBLANES, D_MODEL), F32)

    x = x_ref[...]
    ms = jnp.mean(x * x, axis=-1, keepdims=True)
    hb = (x * lax.rsqrt(ms + EPS) * ng_ref[...]).astype(BF16)

    qkv_scr[SUBLANES:SUBLANES + tb, :] = jnp.dot(hb, wqkv_ref[...], preferred_element_type=F32)
    first = SUBLANES - (DN_CONV - 1)
    outs = (q_ref, k_ref, v_ref)
    for s in range(3 * HEADS):
        c0 = s * HEAD_DIM
        sl = slice(c0, c0 + HEAD_DIM)
        y = qkv_scr[first:first + tb, sl] * cw_ref[0:1, sl]
        for t in range(1, DN_CONV):
            y = y + qkv_scr[first + t:first + t + tb, sl] * cw_ref[t:t + 1, sl]
        y = _silu(y)
        which, hh = divmod(s, HEADS)
        osl = slice(hh * HEAD_DIM, (hh + 1) * HEAD_DIM)
        if which < 2:
            y = y * lax.rsqrt(jnp.sum(y * y, axis=-1, keepdims=True) + EPS)
            if which == 0:
                y = y * (HEAD_DIM ** -0.5)
        outs[which][:, osl] = y
    qkv_scr[first:SUBLANES, :] = qkv_scr[first + tb:SUBLANES + tb, :]

    sz_ref[...] = _silu(jnp.dot(hb, wz_ref[...], preferred_element_type=F32))

    pab = jnp.dot(hb, wab_ref[...], preferred_element_type=F32)
    g = -jnp.exp(alog_ref[...]) * _softplus(pab + dtb_ref[...])
    beta = 1.0 / (1.0 + jnp.exp(-pab))
    lane = lax.broadcasted_iota(jnp.int32, (tb, LANES), 1)
    row = lax.broadcasted_iota(jnp.int32, (tb, LANES), 0) + j * tb
    gb = jnp.where(lane < HEADS, g, beta)
    gb_ref[...] = jnp.where(row >= PAD_FRONT, gb, 0.0)

    pc = jnp.dot(hb, wc_ref[...], preferred_element_type=F32)
    ph = jnp.dot(hb, wh_ref[...], preferred_element_type=F32)
    sc_scr[SUBLANES:SUBLANES + tb, :] = pc * ph
    first = SUBLANES - (SC_CONV - 1)
    conv = sc_scr[first:first + tb, :] * scw_ref[0:1, :]
    for t in range(1, SC_CONV):
        conv = conv + sc_scr[first + t:first + t + tb, :] * scw_ref[t:t + 1, :]
    sc_scr[first:SUBLANES, :] = sc_scr[first + tb:SUBLANES + tb, :]
    pb = jnp.dot(hb, wb_ref[...], preferred_element_type=F32)
    pz = jnp.dot(hb, wzb_ref[...], preferred_element_type=F32)
    osc_ref[...] = pb * conv * _silu(pz)


def _inproj(xp, ng, wqkv, wz, wab, wb, wc, wh, wzb, cw, scw, alog, dtb):
    bsz, lp, _ = xp.shape
    tb = ROW_BLOCK
    row = _row_spec(tb, D_MODEL)
    big = jax.ShapeDtypeStruct((bsz, lp, D_MODEL), F32)
    return pl.pallas_call(
        functools.partial(_inproj_body, tb=tb),
        grid=(bsz, lp // tb),
        in_specs=[row] + [_const_spec(a.shape) for a in
                          (ng, wqkv, wz, wab, wb, wc, wh, wzb, cw, scw, alog, dtb)],
        out_specs=[row, row, row, row, row, _row_spec(tb, LANES)],
        out_shape=[big, big, big, big, big, jax.ShapeDtypeStruct((bsz, lp, LANES), F32)],
        scratch_shapes=[pltpu.VMEM((SUBLANES + tb, 3 * D_MODEL), F32),
                        pltpu.VMEM((SUBLANES + tb, D_MODEL), F32)],
        compiler_params=pltpu.CompilerParams(
            dimension_semantics=("arbitrary", "arbitrary"), vmem_limit_bytes=VMEM_LIMIT),
        name="inproj",
    )(xp, ng, wqkv, wz, wab, wb, wc, wh, wzb, cw, scw, alog, dtb)


def _mm(a, b):
    return jnp.dot(a.astype(BF16), b.astype(BF16), preferred_element_type=F32)


def _chunk_body(q_ref, k_ref, v_ref, gb_ref, c_ref, g_ref, qt_ref, oin_ref, gam_ref, *, nchunk):
    ri = lax.broadcasted_iota(jnp.int32, (CHUNK, CHUNK), 0)
    ci = lax.broadcasted_iota(jnp.int32, (CHUNK, CHUNK), 1)
    causal = ri >= ci
    strict = ri > ci
    ltri = causal.astype(F32)

    def body(c, carry):
        r0 = pl.multiple_of(c * CHUNK, CHUNK)
        rows = pl.ds(r0, CHUNK)
        gbc = gb_ref[rows, :]
        gcum = jnp.dot(ltri, gbc, precision=lax.Precision.HIGHEST,
                       preferred_element_type=F32)
        gcum_t = gcum.T
        glast = gcum[CHUNK - 1:CHUNK, :]
        e_g = jnp.exp(gcum)
        e_kd = jnp.exp(glast - gcum)
        gam_ref[c] = jnp.broadcast_to(jnp.exp(gcum_t[0:HEADS, CHUNK - 1:CHUNK]), (HEADS, LANES))
        for h in range(HEADS):
            sl = slice(h * HEAD_DIM, (h + 1) * HEAD_DIM)
            qh = q_ref[rows, sl]
            kh = k_ref[rows, sl]
            vh = v_ref[rows, sl]
            bcol = gbc[:, HEADS + h:HEADS + h + 1]
            dec = jnp.where(causal, jnp.exp(gcum[:, h:h + 1] - gcum_t[h:h + 1, :]), 0.0)
            kb = kh.astype(BF16)
            qkk = lax.dot_general(jnp.concatenate([qh, kh], axis=0).astype(BF16), kb,
                                  (((1,), (1,)), ((), ())), preferred_element_type=F32)
            qk = qkk[:CHUNK]
            kk = qkk[CHUNK:]
            nm = jnp.where(strict, -(bcol * kk * dec), 0.0)
            xm = nm
            pw = _mm(nm, nm)
            for _ in range(4):
                both = _mm(jnp.concatenate([xm, pw], axis=0), pw)
                xm = xm + pw + both[:CHUNK]
                pw = both[CHUNK:]
            xm = xm + pw + _mm(xm, pw)
            egc = e_g[:, h:h + 1]
            rhs = jnp.concatenate([bcol * vh, (bcol * egc) * kh], axis=1)
            uw = (rhs + _mm(xm, rhs)).astype(BF16)
            kd = (kh * e_kd[:, h:h + 1]).astype(BF16)
            cg = lax.dot_general(kd, uw, (((0,), (0,)), ((), ())), preferred_element_type=F32)
            op = jnp.dot((qk * dec).astype(BF16), uw, preferred_element_type=F32)
            c_ref[c, sl, :] = cg[:, :HEAD_DIM]
            g_ref[c, sl, :] = cg[:, HEAD_DIM:].astype(BF16)
            oin_ref[rows, sl] = op[:, :HEAD_DIM]
            qt_ref[rows, sl] = (qh * egc - op[:, HEAD_DIM:]).astype(BF16)
        return carry

    lax.fori_loop(0, nchunk, body, 0)


def _chunk_call(q, k, v, gb):
    bsz, lp, _ = q.shape
    tb = ROW_BLOCK
    nchunk = tb // CHUNK
    nc_total = lp // CHUNK
    row = _row_spec(tb, D_MODEL)
    mat = pl.BlockSpec((None, nchunk, D_MODEL, HEAD_DIM), lambda b, j: (b, j, 0, 0))
    gam = pl.BlockSpec((None, nchunk, SUBLANES, LANES), lambda b, j: (b, j, 0, 0))
    return pl.pallas_call(
        functools.partial(_chunk_body, nchunk=nchunk),
        grid=(bsz, lp // tb),
        in_specs=[row, row, row, _row_spec(tb, LANES)],
        out_specs=[mat, mat, row, row, gam],
        out_shape=[jax.ShapeDtypeStruct((bsz, nc_total, D_MODEL, HEAD_DIM), F32),
                   jax.ShapeDtypeStruct((bsz, nc_total, D_MODEL, HEAD_DIM), BF16),
                   jax.ShapeDtypeStruct((bsz, lp, D_MODEL), BF16),
                   jax.ShapeDtypeStruct((bsz, lp, D_MODEL), F32),
                   jax.ShapeDtypeStruct((bsz, nc_total, SUBLANES, LANES), F32)],
        compiler_params=pltpu.CompilerParams(
            dimension_semantics=("arbitrary", "arbitrary"), vmem_limit_bytes=VMEM_LIMIT),
        name="chunk",
    )(q, k, v, gb)


def _scan_body(qt_ref, oin_ref, c_ref, g_ref, gam_ref, o_ref, s_scr, *, nchunk):
    @pl.when(pl.program_id(1) == 0)
    def _():
        s_scr[...] = jnp.zeros(s_scr.shape, F32)

    def body(c, carry):
        r0 = pl.multiple_of(c * CHUNK, CHUNK)
        rows = pl.ds(r0, CHUNK)
        gam = gam_ref[c]
        for h in range(HEADS):
            sl = slice(h * HEAD_DIM, (h + 1) * HEAD_DIM)
            s = s_scr[h]
            lhs = jnp.concatenate([g_ref[c, sl, :], qt_ref[rows, sl]], axis=0)
            r = jnp.dot(lhs, s.astype(BF16), preferred_element_type=F32)
            o_ref[rows, sl] = oin_ref[rows, sl] + r[HEAD_DIM:]
            s_scr[h] = gam[h:h + 1, :] * s + c_ref[c, sl, :] - r[:HEAD_DIM]
        return carry

    lax.fori_loop(0, nchunk, body, 0)


def _scan_call(qt, oin, cm, gm, gam):
    bsz, lp, _ = qt.shape
    tb = ROW_BLOCK
    nchunk = tb // CHUNK
    row = _row_spec(tb, D_MODEL)
    mat = pl.BlockSpec((None, nchunk, D_MODEL, HEAD_DIM), lambda b, j: (b, j, 0, 0))
    gsp = pl.BlockSpec((None, nchunk, SUBLANES, LANES), lambda b, j: (b, j, 0, 0))
    return pl.pallas_call(
        functools.partial(_scan_body, nchunk=nchunk),
        grid=(bsz, lp // tb),
        in_specs=[row, row, mat, mat, gsp],
        out_specs=row,
        out_shape=jax.ShapeDtypeStruct((bsz, lp, D_MODEL), F32),
        scratch_shapes=[pltpu.VMEM((HEADS, HEAD_DIM, HEAD_DIM), F32)],
        compiler_params=pltpu.CompilerParams(
            dimension_semantics=("arbitrary", "arbitrary"), vmem_limit_bytes=VMEM_LIMIT),
        name="scan",
    )(qt, oin, cm, gm, gam)


def _outproj_body(x_ref, o_ref, sz_ref, osc_ref, og_ref, wout_ref, fg_ref, y_ref, *, final, row_block):
    j = pl.program_id(1)
    parts = []
    for h in range(HEADS):
        sl = slice(h * HEAD_DIM, (h + 1) * HEAD_DIM)
        o = o_ref[:, sl]
        on = o * lax.rsqrt(jnp.mean(o * o, axis=-1, keepdims=True) + EPS) * og_ref[...]
        parts.append((on * sz_ref[:, sl]).astype(BF16))
    parts.append(osc_ref[...].astype(BF16))
    mix = jnp.concatenate(parts, axis=1)
    y = x_ref[...] + jnp.dot(mix, wout_ref[...], preferred_element_type=F32)
    row = lax.broadcasted_iota(jnp.int32, y.shape, 0) + j * row_block
    y = jnp.where(row >= PAD_FRONT, y, 0.0)
    if final:
        y = y * lax.rsqrt(jnp.mean(y * y, axis=-1, keepdims=True) + EPS) * fg_ref[...]
    y_ref[...] = y


def _outproj(xp, o, sz, osc, og, wout, fg, final):
    bsz, lp, _ = xp.shape
    tb = ROW_BLOCK
    row = _row_spec(tb, D_MODEL)
    return pl.pallas_call(
        functools.partial(_outproj_body, final=final, row_block=tb),
        grid=(bsz, lp // tb),
        in_specs=[row, row, row, row, _const_spec(og.shape), _const_spec(wout.shape),
                  _const_spec(fg.shape)],
        out_specs=row,
        out_shape=jax.ShapeDtypeStruct((bsz, lp, D_MODEL), F32),
        compiler_params=pltpu.CompilerParams(
            dimension_semantics=("arbitrary", "arbitrary"), vmem_limit_bytes=VMEM_LIMIT),
        name="outproj_final" if final else "outproj",
    )(xp, o, sz, osc, og, wout, fg)


def _pad_lanes(v, fill=0.0):
    return jnp.pad(v.astype(F32), (0, LANES - v.shape[0]), constant_values=fill)[None, :]


def kernel(x, meta_tokens, norm_g, w_in, dn_conv_w, dn_A_log, dn_dt_bias, dn_out_g, sc_conv_w,
           w_out, final_g):
    bsz, seq, d = x.shape
    depth = w_in.shape[0]
    dn = HEADS * HEAD_DIM
    sc = (w_in.shape[2] - 4 * dn - 2 * HEADS) // 4
    assert d == D_MODEL and sc == D_MODEL and dn == D_MODEL
    assert (PAD_FRONT + N_META + seq) % ROW_BLOCK == 0

    meta = jnp.broadcast_to(meta_tokens[None].astype(x.dtype), (bsz, N_META, d))
    xp = jnp.concatenate([jnp.zeros((bsz, PAD_FRONT, d), x.dtype), meta, x], axis=1)

    o0 = 4 * dn + 2 * HEADS
    for l in range(depth):
        w = w_in[l]
        wqkv = w[:, :3 * dn].astype(BF16)
        wz = w[:, 3 * dn:4 * dn].astype(BF16)
        wab = jnp.pad(w[:, 4 * dn:o0], ((0, 0), (0, LANES - 2 * HEADS))).astype(BF16)
        wb = w[:, o0:o0 + sc].astype(BF16)
        wc = w[:, o0 + sc:o0 + 2 * sc].astype(BF16)
        wh = w[:, o0 + 2 * sc:o0 + 3 * sc].astype(BF16)
        wzb = w[:, o0 + 3 * sc:o0 + 4 * sc].astype(BF16)
        q, k, v, sz, osc, gb = _inproj(
            xp, norm_g[l][None, :], wqkv, wz, wab, wb, wc, wh, wzb,
            dn_conv_w[l], sc_conv_w[l], _pad_lanes(dn_A_log[l]), _pad_lanes(dn_dt_bias[l]))
        cm, gm, qt, oin, gam = _chunk_call(q, k, v, gb)
        o = _scan_call(qt, oin, cm, gm, gam)
        xp = _outproj(xp, o, sz, osc, dn_out_g[l][None, :], w_out[l].astype(BF16),
                      final_g[None, :], final=(l == depth - 1))
    return xp[:, PAD_FRONT + N_META:]
```

```python
import functools

import jax
import jax.numpy as jnp
from jax import lax
from jax.experimental import pallas as pl
from jax.experimental.pallas import tpu as pltpu

F32 = jnp.float32
BF16 = jnp.bfloat16

D_MODEL = 1024
N_META = 16
HEADS = 8
HEAD_DIM = 128
DN_CONV = 4
SC_CONV = 3
CHUNK = 64
EPS = 1e-6
PAD_FRONT = (-N_META) % CHUNK
LANES = 128
SUBLANES = 8
ROW_BLOCK = 320
VMEM_LIMIT = 56 * 1024 * 1024


def _silu(x):
    return x * (1.0 / (1.0 + jnp.exp(-x)))


def _softplus(x):
    return jnp.maximum(x, 0.0) + jnp.log(1.0 + jnp.exp(-jnp.abs(x)))


def _const_spec(shape):
    nd = len(shape)
    return pl.BlockSpec(shape, lambda *_: (0,) * nd, pipeline_mode=pl.Buffered(1))


def _row_spec(rows, cols):
    return pl.BlockSpec((None, rows, cols), lambda b, j: (b, j, 0))


def _inproj_body(x_ref, ng_ref, wqkv_ref, wz_ref, wab_ref, wb_ref, wc_ref, wh_ref, wzb_ref,
                 cw_ref, scw_ref, alog_ref, dtb_ref,
                 q_ref, k_ref, v_ref, sz_ref, osc_ref, gb_ref,
                 qkv_scr, sc_scr, *, tb):
    j = pl.program_id(1)

    @pl.when(j == 0)
    def _():
        qkv_scr[0:SUBLANES, :] = jnp.zeros((SUBLANES, 3 * D_MODEL), F32)
        sc_scr[0:SUBLANES, :] = jnp.zeros((SUBLANES, D_MODEL), F32)

    x = x_ref[...]
    ms = jnp.mean(x * x, axis=-1, keepdims=True)
    hb = (x * lax.rsqrt(ms + EPS) * ng_ref[...]).astype(BF16)

    qkv_scr[SUBLANES:SUBLANES + tb, :] = jnp.dot(hb, wqkv_ref[...], preferred_element_type=F32)
    first = SUBLANES - (DN_CONV - 1)
    outs = (q_ref, k_ref, v_ref)
    for s in range(3 * HEADS):
        c0 = s * HEAD_DIM
        sl = slice(c0, c0 + HEAD_DIM)
        y = qkv_scr[first:first + tb, sl] * cw_ref[0:1, sl]
        for t in range(1, DN_CONV):
            y = y + qkv_scr[first + t:first + t + tb, sl] * cw_ref[t:t + 1, sl]
        y = _silu(y)
        which, hh = divmod(s, HEADS)
        osl = slice(hh * HEAD_DIM, (hh + 1) * HEAD_DIM)
        if which < 2:
            y = y * lax.rsqrt(jnp.sum(y * y, axis=-1, keepdims=True) + EPS)
            if which == 0:
                y = y * (HEAD_DIM ** -0.5)
        outs[which][:, osl] = y
    qkv_scr[first:SUBLANES, :] = qkv_scr[first + tb:SUBLANES + tb, :]

    sz_ref[...] = _silu(jnp.dot(hb, wz_ref[...], preferred_element_type=F32))

    pab = jnp.dot(hb, wab_ref[...], preferred_element_type=F32)
    g = -jnp.exp(alog_ref[...]) * _softplus(pab + dtb_ref[...])
    beta = 1.0 / (1.0 + jnp.exp(-pab))
    lane = lax.broadcasted_iota(jnp.int32, (tb, LANES), 1)
    row = lax.broadcasted_iota(jnp.int32, (tb, LANES), 0) + j * tb
    gb = jnp.where(lane < HEADS, g, beta)
    gb_ref[...] = jnp.where(row >= PAD_FRONT, gb, 0.0)

    pc = jnp.dot(hb, wc_ref[...], preferred_element_type=F32)
    ph = jnp.dot(hb, wh_ref[...], preferred_element_type=F32)
    sc_scr[SUBLANES:SUBLANES + tb, :] = pc * ph
    first = SUBLANES - (SC_CONV - 1)
    conv = sc_scr[first:first + tb, :] * scw_ref[0:1, :]
    for t in range(1, SC_CONV):
        conv = conv + sc_scr[first + t:first + t + tb, :] * scw_ref[t:t + 1, :]
    sc_scr[first:SUBLANES, :] = sc_scr[first + tb:SUBLANES + tb, :]
    pb = jnp.dot(hb, wb_ref[...], preferred_element_type=F32)
    pz = jnp.dot(hb, wzb_ref[...], preferred_element_type=F32)
    osc_ref[...] = pb * conv * _silu(pz)


def _inproj(xp, ng, wqkv, wz, wab, wb, wc, wh, wzb, cw, scw, alog, dtb):
    bsz, lp, _ = xp.shape
    tb = ROW_BLOCK
    row = _row_spec(tb, D_MODEL)
    big = jax.ShapeDtypeStruct((bsz, lp, D_MODEL), F32)
    return pl.pallas_call(
        functools.partial(_inproj_body, tb=tb),
        grid=(bsz, lp // tb),
        in_specs=[row] + [_const_spec(a.shape) for a in
                          (ng, wqkv, wz, wab, wb, wc, wh, wzb, cw, scw, alog, dtb)],
        out_specs=[row, row, row, row, row, _row_spec(tb, LANES)],
        out_shape=[big, big, big, big, big, jax.ShapeDtypeStruct((bsz, lp, LANES), F32)],
        scratch_shapes=[pltpu.VMEM((SUBLANES + tb, 3 * D_MODEL), F32),
                        pltpu.VMEM((SUBLANES + tb, D_MODEL), F32)],
        compiler_params=pltpu.CompilerParams(
            dimension_semantics=("arbitrary", "arbitrary"), vmem_limit_bytes=VMEM_LIMIT),
        name="inproj",
    )(xp, ng, wqkv, wz, wab, wb, wc, wh, wzb, cw, scw, alog, dtb)


def _mm(a, b):
    return jnp.dot(a.astype(BF16), b.astype(BF16), preferred_element_type=F32)


def _chunk_body(q_ref, k_ref, v_ref, gb_ref, c_ref, g_ref, qt_ref, oin_ref, gam_ref, *, nchunk):
    ri = lax.broadcasted_iota(jnp.int32, (CHUNK, CHUNK), 0)
    ci = lax.broadcasted_iota(jnp.int32, (CHUNK, CHUNK), 1)
    causal = ri >= ci
    strict = ri > ci
    ltri = causal.astype(F32)

    probs = [(c, h) for c in range(nchunk) for h in range(HEADS)]
    rows = [slice(c * CHUNK, (c + 1) * CHUNK) for c in range(nchunk)]
    sls = [slice(h * HEAD_DIM, (h + 1) * HEAD_DIM) for h in range(HEADS)]
    gbc, gcum, gcum_t, e_g, e_kd = [], [], [], [], []
    for c in range(nchunk):
        gbc.append(gb_ref[rows[c], :])
        gcum.append(jnp.dot(ltri, gbc[c], precision=lax.Precision.HIGHEST,
                            preferred_element_type=F32))
        gcum_t.append(gcum[c].T)
        e_g.append(jnp.exp(gcum[c]))
        e_kd.append(jnp.exp(gcum[c][CHUNK - 1:CHUNK, :] - gcum[c]))
        gam_ref[c] = jnp.broadcast_to(jnp.exp(gcum_t[c][0:HEADS, CHUNK - 1:CHUNK]), (HEADS, LANES))
    qh = [q_ref[rows[c], sls[h]] for c, h in probs]
    kh = [k_ref[rows[c], sls[h]] for c, h in probs]
    bcol = [gbc[c][:, HEADS + h:HEADS + h + 1] for c, h in probs]
    egc = [e_g[c][:, h:h + 1] for c, h in probs]
    dec = [jnp.where(causal, jnp.exp(gcum[c][:, h:h + 1] - gcum_t[c][h:h + 1, :]), 0.0)
           for c, h in probs]
    ps = range(len(probs))
    qkk = [lax.dot_general(jnp.concatenate([qh[p], kh[p]], axis=0).astype(BF16),
                           kh[p].astype(BF16), (((1,), (1,)), ((), ())),
                           preferred_element_type=F32) for p in ps]
    xm = [jnp.where(strict, -(bcol[p] * qkk[p][CHUNK:] * dec[p]), 0.0) for p in ps]
    pw = [_mm(xm[p], xm[p]) for p in ps]
    for _ in range(4):
        both = [_mm(jnp.concatenate([xm[p], pw[p]], axis=0), pw[p]) for p in ps]
        xm = [xm[p] + pw[p] + both[p][:CHUNK] for p in ps]
        pw = [both[p][CHUNK:] for p in ps]
    last = [_mm(xm[p], pw[p]) for p in ps]
    xm = [xm[p] + pw[p] + last[p] for p in ps]
    rhs = [jnp.concatenate([bcol[p] * v_ref[rows[c], sls[h]], (bcol[p] * egc[p]) * kh[p]], axis=1)
           for p, (c, h) in enumerate(probs)]
    xr = [_mm(xm[p], rhs[p]) for p in ps]
    uw = [(rhs[p] + xr[p]).astype(BF16) for p in ps]
    cg = [lax.dot_general((kh[p] * e_kd[c][:, h:h + 1]).astype(BF16), uw[p],
                          (((0,), (0,)), ((), ())), preferred_element_type=F32)
          for p, (c, h) in enumerate(probs)]
    op = [jnp.dot((qkk[p][:CHUNK] * dec[p]).astype(BF16), uw[p], preferred_element_type=F32)
          for p in ps]
    for p, (c, h) in enumerate(probs):
        c_ref[c, sls[h], :] = cg[p][:, :HEAD_DIM]
        g_ref[c, sls[h], :] = cg[p][:, HEAD_DIM:].astype(BF16)
        oin_ref[rows[c], sls[h]] = op[p][:, :HEAD_DIM]
        qt_ref[rows[c], sls[h]] = (qh[p] * egc[p] - op[p][:, HEAD_DIM:]).astype(BF16)


def _chunk_call(q, k, v, gb):
    bsz, lp, _ = q.shape
    tb = ROW_BLOCK
    nchunk = tb // CHUNK
    nc_total = lp // CHUNK
    row = _row_spec(tb, D_MODEL)
    mat = pl.BlockSpec((None, nchunk, D_MODEL, HEAD_DIM), lambda b, j: (b, j, 0, 0))
    gam = pl.BlockSpec((None, nchunk, SUBLANES, LANES), lambda b, j: (b, j, 0, 0))
    return pl.pallas_call(
        functools.partial(_chunk_body, nchunk=nchunk),
        grid=(bsz, lp // tb),
        in_specs=[row, row, row, _row_spec(tb, LANES)],
        out_specs=[mat, mat, row, row, gam],
        out_shape=[jax.ShapeDtypeStruct((bsz, nc_total, D_MODEL, HEAD_DIM), F32),
                   jax.ShapeDtypeStruct((bsz, nc_total, D_MODEL, HEAD_DIM), BF16),
                   jax.ShapeDtypeStruct((bsz, lp, D_MODEL), BF16),
                   jax.ShapeDtypeStruct((bsz, lp, D_MODEL), F32),
                   jax.ShapeDtypeStruct((bsz, nc_total, SUBLANES, LANES), F32)],
        compiler_params=pltpu.CompilerParams(
            dimension_semantics=("arbitrary", "arbitrary"), vmem_limit_bytes=VMEM_LIMIT),
        name="chunk",
    )(q, k, v, gb)


def _scan_body(qt_ref, oin_ref, c_ref, g_ref, gam_ref, o_ref, s_scr, *, nchunk):
    @pl.when(pl.program_id(1) == 0)
    def _():
        s_scr[...] = jnp.zeros(s_scr.shape, F32)

    def body(c, carry):
        r0 = pl.multiple_of(c * CHUNK, CHUNK)
        rows = pl.ds(r0, CHUNK)
        gam = gam_ref[c]
        for h in range(HEADS):
            sl = slice(h * HEAD_DIM, (h + 1) * HEAD_DIM)
            s = s_scr[h]
            lhs = jnp.concatenate([g_ref[c, sl, :], qt_ref[rows, sl]], axis=0)
            r = jnp.dot(lhs, s.astype(BF16), preferred_element_type=F32)
            o_ref[rows, sl] = oin_ref[rows, sl] + r[HEAD_DIM:]
            s_scr[h] = gam[h:h + 1, :] * s + c_ref[c, sl, :] - r[:HEAD_DIM]
        return carry

    lax.fori_loop(0, nchunk, body, 0)


def _scan_call(qt, oin, cm, gm, gam):
    bsz, lp, _ = qt.shape
    tb = ROW_BLOCK
    nchunk = tb // CHUNK
    row = _row_spec(tb, D_MODEL)
    mat = pl.BlockSpec((None, nchunk, D_MODEL, HEAD_DIM), lambda b, j: (b, j, 0, 0))
    gsp = pl.BlockSpec((None, nchunk, SUBLANES, LANES), lambda b, j: (b, j, 0, 0))
    return pl.pallas_call(
        functools.partial(_scan_body, nchunk=nchunk),
        grid=(bsz, lp // tb),
        in_specs=[row, row, mat, mat, gsp],
        out_specs=row,
        out_shape=jax.ShapeDtypeStruct((bsz, lp, D_MODEL), F32),
        scratch_shapes=[pltpu.VMEM((HEADS, HEAD_DIM, HEAD_DIM), F32)],
        compiler_params=pltpu.CompilerParams(
            dimension_semantics=("arbitrary", "arbitrary"), vmem_limit_bytes=VMEM_LIMIT),
        name="scan",
    )(qt, oin, cm, gm, gam)


def _outproj_body(x_ref, o_ref, sz_ref, osc_ref, og_ref, wout_ref, fg_ref, y_ref, *, final, row_block):
    j = pl.program_id(1)
    parts = []
    for h in range(HEADS):
        sl = slice(h * HEAD_DIM, (h + 1) * HEAD_DIM)
        o = o_ref[:, sl]
        on = o * lax.rsqrt(jnp.mean(o * o, axis=-1, keepdims=True) + EPS) * og_ref[...]
        parts.append((on * sz_ref[:, sl]).astype(BF16))
    parts.append(osc_ref[...].astype(BF16))
    mix = jnp.concatenate(parts, axis=1)
    y = x_ref[...] + jnp.dot(mix, wout_ref[...], preferred_element_type=F32)
    row = lax.broadcasted_iota(jnp.int32, y.shape, 0) + j * row_block
    y = jnp.where(row >= PAD_FRONT, y, 0.0)
    if final:
        y = y * lax.rsqrt(jnp.mean(y * y, axis=-1, keepdims=True) + EPS) * fg_ref[...]
    y_ref[...] = y


def _outproj(xp, o, sz, osc, og, wout, fg, final):
    bsz, lp, _ = xp.shape
    tb = ROW_BLOCK
    row = _row_spec(tb, D_MODEL)
    return pl.pallas_call(
        functools.partial(_outproj_body, final=final, row_block=tb),
        grid=(bsz, lp // tb),
        in_specs=[row, row, row, row, _const_spec(og.shape), _const_spec(wout.shape),
                  _const_spec(fg.shape)],
        out_specs=row,
        out_shape=jax.ShapeDtypeStruct((bsz, lp, D_MODEL), F32),
        compiler_params=pltpu.CompilerParams(
            dimension_semantics=("arbitrary", "arbitrary"), vmem_limit_bytes=VMEM_LIMIT),
        name="outproj_final" if final else "outproj",
    )(xp, o, sz, osc, og, wout, fg)


def _pad_lanes(v, fill=0.0):
    return jnp.pad(v.astype(F32), (0, LANES - v.shape[0]), constant_values=fill)[None, :]


def kernel(x, meta_tokens, norm_g, w_in, dn_conv_w, dn_A_log, dn_dt_bias, dn_out_g, sc_conv_w,
           w_out, final_g):
    bsz, seq, d = x.shape
    depth = w_in.shape[0]
    dn = HEADS * HEAD_DIM
    sc = (w_in.shape[2] - 4 * dn - 2 * HEADS) // 4
    assert d == D_MODEL and sc == D_MODEL and dn == D_MODEL
    assert (PAD_FRONT + N_META + seq) % ROW_BLOCK == 0

    meta = jnp.broadcast_to(meta_tokens[None].astype(x.dtype), (bsz, N_META, d))
    xp = jnp.concatenate([jnp.zeros((bsz, PAD_FRONT, d), x.dtype), meta, x], axis=1)

    o0 = 4 * dn + 2 * HEADS
    for l in range(depth):
        w = w_in[l]
        wqkv = w[:, :3 * dn].astype(BF16)
        wz = w[:, 3 * dn:4 * dn].astype(BF16)
        wab = jnp.pad(w[:, 4 * dn:o0], ((0, 0), (0, LANES - 2 * HEADS))).astype(BF16)
        wb = w[:, o0:o0 + sc].astype(BF16)
        wc = w[:, o0 + sc:o0 + 2 * sc].astype(BF16)
        wh = w[:, o0 + 2 * sc:o0 + 3 * sc].astype(BF16)
        wzb = w[:, o0 + 3 * sc:o0 + 4 * sc].astype(BF16)
        q, k, v, sz, osc, gb = _inproj(
            xp, norm_g[l][None, :], wqkv, wz, wab, wb, wc, wh, wzb,
            dn_conv_w[l], sc_conv_w[l], _pad_lanes(dn_A_log[l]), _pad_lanes(dn_dt_bias[l]))
        cm, gm, qt, oin, gam = _chunk_call(q, k, v, gb)
        o = _scan_call(qt, oin, cm, gm, gam)
        xp = _outproj(xp, o, sz, osc, dn_out_g[l][None, :], w_out[l].astype(BF16),
                      final_g[None, :], final=(l == depth - 1))
    return xp[:, PAD_FRONT + N_META:]
```

```python
import functools

import jax
import jax.numpy as jnp
from jax import lax
from jax.experimental import pallas as pl
from jax.experimental.pallas import tpu as pltpu

F32 = jnp.float32
BF16 = jnp.bfloat16

D_MODEL = 1024
N_META = 16
HEADS = 8
HEAD_DIM = 128
DN_CONV = 4
SC_CONV = 3
CHUNK = 64
EPS = 1e-6
PAD_FRONT = (-N_META) % CHUNK
LANES = 128
SUBLANES = 8
ROW_BLOCK = 320
INPROJ_ROWS = 416
VMEM_LIMIT = 56 * 1024 * 1024


def _silu(x):
    return x * (1.0 / (1.0 + jnp.exp(-x)))


def _softplus(x):
    return jnp.maximum(x, 0.0) + jnp.log(1.0 + jnp.exp(-jnp.abs(x)))


def _const_spec(shape):
    nd = len(shape)
    return pl.BlockSpec(shape, lambda *_: (0,) * nd, pipeline_mode=pl.Buffered(1))


def _row_spec(rows, cols):
    return pl.BlockSpec((None, rows, cols), lambda b, j: (b, j, 0))


def _inproj_body(x_ref, ng_ref, wqkv_ref, wz_ref, wab_ref, wb_ref, wc_ref, wh_ref, wzb_ref,
                 cw_ref, scw_ref, alog_ref, dtb_ref,
                 q_ref, k_ref, v_ref, sz_ref, osc_ref, gb_ref,
                 qkv_scr, sc_scr, conv_scr, *, tb):
    j = pl.program_id(1)
    tbp = tb // SUBLANES
    wide = 2 * LANES

    @pl.when(j == 0)
    def _():
        qkv_scr[:, 0:SUBLANES, :] = jnp.zeros((3 * HEADS, SUBLANES, LANES), F32)
        sc_scr[:, 0:SUBLANES, :] = jnp.zeros((HEADS, SUBLANES, LANES), F32)

    x = x_ref[...]
    ms = jnp.mean(x * x, axis=-1, keepdims=True)
    hb = (x * lax.rsqrt(ms + EPS) * ng_ref[...]).astype(BF16)

    def proj(w_ref, n):
        return jnp.dot(hb, w_ref[:, n * wide:(n + 1) * wide], preferred_element_type=F32)

    def strided(ref, slab, start):
        return ref[slab, pl.ds(start, SUBLANES, stride=tbp), :]

    for n in range(3 * D_MODEL // wide):
        r = proj(wqkv_ref, n)
        qkv_scr[2 * n, SUBLANES:SUBLANES + tb, :] = r[:, :LANES]
        qkv_scr[2 * n + 1, SUBLANES:SUBLANES + tb, :] = r[:, LANES:]
    first = SUBLANES - (DN_CONV - 1)
    outs = (q_ref, k_ref, v_ref)
    for s in range(3 * HEADS):
        which, hh = divmod(s, HEADS)
        sl = slice(s * LANES, (s + 1) * LANES)
        w = [jnp.broadcast_to(cw_ref[t:t + 1, sl], (SUBLANES, LANES)) for t in range(DN_CONV)]
        for v in range(tbp):
            y = w[0] * strided(qkv_scr, s, first + v)
            for t in range(1, DN_CONV):
                y = y + w[t] * strided(qkv_scr, s, first + v + t)
            y = _silu(y)
            if which < 2:
                y = y * lax.rsqrt(jnp.sum(y * y, axis=-1, keepdims=True) + EPS)
                if which == 0:
                    y = y * (HEAD_DIM ** -0.5)
            outs[which][hh, pl.ds(v, SUBLANES, stride=tbp), :] = y
    qkv_scr[:, first:SUBLANES, :] = qkv_scr[:, first + tb:SUBLANES + tb, :]

    for n in range(D_MODEL // wide):
        sz_ref[:, n * wide:(n + 1) * wide] = _silu(proj(wz_ref, n))

    pab = jnp.dot(hb, wab_ref[...], preferred_element_type=F32)
    g = -jnp.exp(alog_ref[...]) * _softplus(pab + dtb_ref[...])
    beta = 1.0 / (1.0 + jnp.exp(-pab))
    lane = lax.broadcasted_iota(jnp.int32, (tb, LANES), 1)
    row = lax.broadcasted_iota(jnp.int32, (tb, LANES), 0) + j * tb
    gb = jnp.where(lane < HEADS, g, beta)
    gb_ref[...] = jnp.where(row >= PAD_FRONT, gb, 0.0)

    for n in range(D_MODEL // wide):
        u = proj(wc_ref, n) * proj(wh_ref, n)
        sc_scr[2 * n, SUBLANES:SUBLANES + tb, :] = u[:, :LANES]
        sc_scr[2 * n + 1, SUBLANES:SUBLANES + tb, :] = u[:, LANES:]
    first = SUBLANES - (SC_CONV - 1)
    for s in range(D_MODEL // LANES):
        sl = slice(s * LANES, (s + 1) * LANES)
        w = [jnp.broadcast_to(scw_ref[t:t + 1, sl], (SUBLANES, LANES)) for t in range(SC_CONV)]
        for v in range(tbp):
            y = w[0] * strided(sc_scr, s, first + v)
            for t in range(1, SC_CONV):
                y = y + w[t] * strided(sc_scr, s, first + v + t)
            conv_scr[s, pl.ds(v, SUBLANES, stride=tbp), :] = y
    sc_scr[:, first:SUBLANES, :] = sc_scr[:, first + tb:SUBLANES + tb, :]
    for n in range(D_MODEL // wide):
        conv = jnp.concatenate([conv_scr[2 * n], conv_scr[2 * n + 1]], axis=1)
        y = proj(wb_ref, n) * conv * _silu(proj(wzb_ref, n))
        osc_ref[:, n * wide:(n + 1) * wide] = y.astype(BF16)


def _inproj(xp, ng, wqkv, wz, wab, wb, wc, wh, wzb, cw, scw, alog, dtb):
    bsz, lp, _ = xp.shape
    tb = INPROJ_ROWS
    row = _row_spec(tb, D_MODEL)
    heads = pl.BlockSpec((None, HEADS, tb, HEAD_DIM), lambda b, j: (b, 0, j, 0))
    per_head = jax.ShapeDtypeStruct((bsz, HEADS, lp, HEAD_DIM), F32)
    return pl.pallas_call(
        functools.partial(_inproj_body, tb=tb),
        grid=(bsz, lp // tb),
        in_specs=[row] + [_const_spec(a.shape) for a in
                          (ng, wqkv, wz, wab, wb, wc, wh, wzb, cw, scw, alog, dtb)],
        out_specs=[heads, heads, heads, row, row, _row_spec(tb, LANES)],
        out_shape=[per_head, per_head, per_head,
                   jax.ShapeDtypeStruct((bsz, lp, D_MODEL), F32),
                   jax.ShapeDtypeStruct((bsz, lp, D_MODEL), BF16),
                   jax.ShapeDtypeStruct((bsz, lp, LANES), F32)],
        scratch_shapes=[pltpu.VMEM((3 * HEADS, SUBLANES + tb, LANES), F32),
                        pltpu.VMEM((HEADS, SUBLANES + tb, LANES), F32),
                        pltpu.VMEM((HEADS, tb, LANES), F32)],
        compiler_params=pltpu.CompilerParams(
            dimension_semantics=("arbitrary", "arbitrary"), vmem_limit_bytes=VMEM_LIMIT),
        name="inproj",
    )(xp, ng, wqkv, wz, wab, wb, wc, wh, wzb, cw, scw, alog, dtb)


def _mm(a, b):
    return jnp.dot(a.astype(BF16), b.astype(BF16), preferred_element_type=F32)


def _chunk_body(q_ref, k_ref, v_ref, gb_ref, c_ref, g_ref, qt_ref, oin_ref, gam_ref, *, nchunk):
    ri = lax.broadcasted_iota(jnp.int32, (CHUNK, CHUNK), 0)
    ci = lax.broadcasted_iota(jnp.int32, (CHUNK, CHUNK), 1)
    causal = ri >= ci
    strict = ri > ci
    ltri = causal.astype(F32)

    probs = [(c, h) for c in range(nchunk) for h in range(HEADS)]
    rows = [slice(c * CHUNK, (c + 1) * CHUNK) for c in range(nchunk)]
    sls = [slice(h * HEAD_DIM, (h + 1) * HEAD_DIM) for h in range(HEADS)]
    gbc, gcum, gcum_t, e_g, e_kd = [], [], [], [], []
    for c in range(nchunk):
        gbc.append(gb_ref[rows[c], :])
        gcum.append(jnp.dot(ltri, gbc[c], precision=lax.Precision.HIGHEST,
                            preferred_element_type=F32))
        gcum_t.append(gcum[c].T)
        e_g.append(jnp.exp(gcum[c]))
        e_kd.append(jnp.exp(gcum[c][CHUNK - 1:CHUNK, :] - gcum[c]))
        gam_ref[c] = jnp.broadcast_to(jnp.exp(gcum_t[c][0:HEADS, CHUNK - 1:CHUNK]), (HEADS, LANES))
    qh = [q_ref[h, rows[c], :] for c, h in probs]
    kh = [k_ref[h, rows[c], :] for c, h in probs]
    bcol = [gbc[c][:, HEADS + h:HEADS + h + 1] for c, h in probs]
    egc = [e_g[c][:, h:h + 1] for c, h in probs]
    dec = [jnp.where(causal, jnp.exp(gcum[c][:, h:h + 1] - gcum_t[c][h:h + 1, :]), 0.0)
           for c, h in probs]
    ps = range(len(probs))
    qkk = [lax.dot_general(jnp.concatenate([qh[p], kh[p]], axis=0).astype(BF16),
                           kh[p].astype(BF16), (((1,), (1,)), ((), ())),
                           preferred_element_type=F32) for p in ps]
    xm = [jnp.where(strict, -(bcol[p] * qkk[p][CHUNK:] * dec[p]), 0.0) for p in ps]
    pw = [_mm(xm[p], xm[p]) for p in ps]
    for _ in range(4):
        both = [_mm(jnp.concatenate([xm[p], pw[p]], axis=0), pw[p]) for p in ps]
        xm = [xm[p] + pw[p] + both[p][:CHUNK] for p in ps]
        pw = [both[p][CHUNK:] for p in ps]
    last = [_mm(xm[p], pw[p]) for p in ps]
    xm = [xm[p] + pw[p] + last[p] for p in ps]
    rhs = [jnp.concatenate([bcol[p] * v_ref[h, rows[c], :], (bcol[p] * egc[p]) * kh[p]], axis=1)
           for p, (c, h) in enumerate(probs)]
    xr = [_mm(xm[p], rhs[p]) for p in ps]
    uw = [(rhs[p] + xr[p]).astype(BF16) for p in ps]
    cg = [lax.dot_general((kh[p] * e_kd[c][:, h:h + 1]).astype(BF16), uw[p],
                          (((0,), (0,)), ((), ())), preferred_element_type=F32)
          for p, (c, h) in enumerate(probs)]
    op = [jnp.dot((qkk[p][:CHUNK] * dec[p]).astype(BF16), uw[p], preferred_element_type=F32)
          for p in ps]
    for p, (c, h) in enumerate(probs):
        c_ref[c, sls[h], :] = cg[p][:, :HEAD_DIM]
        g_ref[c, sls[h], :] = cg[p][:, HEAD_DIM:].astype(BF16)
        oin_ref[rows[c], sls[h]] = op[p][:, :HEAD_DIM]
        qt_ref[rows[c], sls[h]] = (qh[p] * egc[p] - op[p][:, HEAD_DIM:]).astype(BF16)


def _chunk_call(q, k, v, gb):
    bsz, _, lp, _ = q.shape
    tb = ROW_BLOCK
    nchunk = tb // CHUNK
    nc_total = lp // CHUNK
    row = _row_spec(tb, D_MODEL)
    heads = pl.BlockSpec((None, HEADS, tb, HEAD_DIM), lambda b, j: (b, 0, j, 0))
    mat = pl.BlockSpec((None, nchunk, D_MODEL, HEAD_DIM), lambda b, j: (b, j, 0, 0))
    gam = pl.BlockSpec((None, nchunk, SUBLANES, LANES), lambda b, j: (b, j, 0, 0))
    return pl.pallas_call(
        functools.partial(_chunk_body, nchunk=nchunk),
        grid=(bsz, lp // tb),
        in_specs=[heads, heads, heads, _row_spec(tb, LANES)],
        out_specs=[mat, mat, row, row, gam],
        out_shape=[jax.ShapeDtypeStruct((bsz, nc_total, D_MODEL, HEAD_DIM), F32),
                   jax.ShapeDtypeStruct((bsz, nc_total, D_MODEL, HEAD_DIM), BF16),
                   jax.ShapeDtypeStruct((bsz, lp, D_MODEL), BF16),
                   jax.ShapeDtypeStruct((bsz, lp, D_MODEL), F32),
                   jax.ShapeDtypeStruct((bsz, nc_total, SUBLANES, LANES), F32)],
        compiler_params=pltpu.CompilerParams(
            dimension_semantics=("arbitrary", "arbitrary"), vmem_limit_bytes=VMEM_LIMIT),
        name="chunk",
    )(q, k, v, gb)


def _scan_body(qt_ref, oin_ref, c_ref, g_ref, gam_ref, o_ref, s_scr, *, nchunk):
    @pl.when(pl.program_id(1) == 0)
    def _():
        s_scr[...] = jnp.zeros(s_scr.shape, F32)

    def body(c, carry):
        r0 = pl.multiple_of(c * CHUNK, CHUNK)
        rows = pl.ds(r0, CHUNK)
        gam = gam_ref[c]
        for h in range(HEADS):
            sl = slice(h * HEAD_DIM, (h + 1) * HEAD_DIM)
            s = s_scr[h]
            lhs = jnp.concatenate([g_ref[c, sl, :], qt_ref[rows, sl]], axis=0)
            r = jnp.dot(lhs, s.astype(BF16), preferred_element_type=F32)
            o_ref[rows, sl] = oin_ref[rows, sl] + r[HEAD_DIM:]
            s_scr[h] = gam[h:h + 1, :] * s + c_ref[c, sl, :] - r[:HEAD_DIM]
        return carry

    lax.fori_loop(0, nchunk, body, 0)


def _scan_call(qt, oin, cm, gm, gam):
    bsz, lp, _ = qt.shape
    tb = ROW_BLOCK
    nchunk = tb // CHUNK
    row = _row_spec(tb, D_MODEL)
    mat = pl.BlockSpec((None, nchunk, D_MODEL, HEAD_DIM), lambda b, j: (b, j, 0, 0))
    gsp = pl.BlockSpec((None, nchunk, SUBLANES, LANES), lambda b, j: (b, j, 0, 0))
    return pl.pallas_call(
        functools.partial(_scan_body, nchunk=nchunk),
        grid=(bsz, lp // tb),
        in_specs=[row, row, mat, mat, gsp],
        out_specs=row,
        out_shape=jax.ShapeDtypeStruct((bsz, lp, D_MODEL), F32),
        scratch_shapes=[pltpu.VMEM((HEADS, HEAD_DIM, HEAD_DIM), F32)],
        compiler_params=pltpu.CompilerParams(
            dimension_semantics=("arbitrary", "arbitrary"), vmem_limit_bytes=VMEM_LIMIT),
        name="scan",
    )(qt, oin, cm, gm, gam)


def _outproj_body(x_ref, o_ref, sz_ref, osc_ref, og_ref, wout_ref, fg_ref, y_ref, *, final, row_block):
    j = pl.program_id(1)
    parts = []
    for h in range(HEADS):
        sl = slice(h * HEAD_DIM, (h + 1) * HEAD_DIM)
        o = o_ref[:, sl]
        on = o * lax.rsqrt(jnp.mean(o * o, axis=-1, keepdims=True) + EPS) * og_ref[...]
        parts.append((on * sz_ref[:, sl]).astype(BF16))
    parts.append(osc_ref[...])
    mix = jnp.concatenate(parts, axis=1)
    y = x_ref[...] + jnp.dot(mix, wout_ref[...], preferred_element_type=F32)
    row = lax.broadcasted_iota(jnp.int32, y.shape, 0) + j * row_block
    y = jnp.where(row >= PAD_FRONT, y, 0.0)
    if final:
        y = y * lax.rsqrt(jnp.mean(y * y, axis=-1, keepdims=True) + EPS) * fg_ref[...]
    y_ref[...] = y


def _outproj(xp, o, sz, osc, og, wout, fg, final):
    bsz, lp, _ = xp.shape
    tb = ROW_BLOCK
    row = _row_spec(tb, D_MODEL)
    return pl.pallas_call(
        functools.partial(_outproj_body, final=final, row_block=tb),
        grid=(bsz, lp // tb),
        in_specs=[row, row, row, row, _const_spec(og.shape), _const_spec(wout.shape),
                  _const_spec(fg.shape)],
        out_specs=row,
        out_shape=jax.ShapeDtypeStruct((bsz, lp, D_MODEL), F32),
        compiler_params=pltpu.CompilerParams(
            dimension_semantics=("arbitrary", "arbitrary"), vmem_limit_bytes=VMEM_LIMIT),
        name="outproj_final" if final else "outproj",
    )(xp, o, sz, osc, og, wout, fg)


def _pad_lanes(v, fill=0.0):
    return jnp.pad(v.astype(F32), (0, LANES - v.shape[0]), constant_values=fill)[None, :]


def kernel(x, meta_tokens, norm_g, w_in, dn_conv_w, dn_A_log, dn_dt_bias, dn_out_g, sc_conv_w,
           w_out, final_g):
    bsz, seq, d = x.shape
    depth = w_in.shape[0]
    dn = HEADS * HEAD_DIM
    sc = (w_in.shape[2] - 4 * dn - 2 * HEADS) // 4
    assert d == D_MODEL and sc == D_MODEL and dn == D_MODEL
    lp = PAD_FRONT + N_META + seq
    assert lp % ROW_BLOCK == 0 and lp % INPROJ_ROWS == 0

    meta = jnp.broadcast_to(meta_tokens[None].astype(x.dtype), (bsz, N_META, d))
    xp = jnp.concatenate([jnp.zeros((bsz, PAD_FRONT, d), x.dtype), meta, x], axis=1)

    o0 = 4 * dn + 2 * HEADS
    for l in range(depth):
        w = w_in[l]
        wqkv = w[:, :3 * dn].astype(BF16)
        wz = w[:, 3 * dn:4 * dn].astype(BF16)
        wab = jnp.pad(w[:, 4 * dn:o0], ((0, 0), (0, LANES - 2 * HEADS))).astype(BF16)
        wb = w[:, o0:o0 + sc].astype(BF16)
        wc = w[:, o0 + sc:o0 + 2 * sc].astype(BF16)
        wh = w[:, o0 + 2 * sc:o0 + 3 * sc].astype(BF16)
        wzb = w[:, o0 + 3 * sc:o0 + 4 * sc].astype(BF16)
        q, k, v, sz, osc, gb = _inproj(
            xp, norm_g[l][None, :], wqkv, wz, wab, wb, wc, wh, wzb,
            dn_conv_w[l], sc_conv_w[l], _pad_lanes(dn_A_log[l]), _pad_lanes(dn_dt_bias[l]))
        cm, gm, qt, oin, gam = _chunk_call(q, k, v, gb)
        o = _scan_call(qt, oin, cm, gm, gam)
        xp = _outproj(xp, o, sz, osc, dn_out_g[l][None, :], w_out[l].astype(BF16),
                      final_g[None, :], final=(l == depth - 1))
    return xp[:, PAD_FRONT + N_META:]
```

```python
import functools

import jax
import jax.numpy as jnp
from jax import lax
from jax.experimental import pallas as pl
from jax.experimental.pallas import tpu as pltpu

F32 = jnp.float32
BF16 = jnp.bfloat16

D_MODEL = 1024
N_META = 16
HEADS = 8
HEAD_DIM = 128
DN_CONV = 4
SC_CONV = 3
CHUNK = 64
EPS = 1e-6
PAD_FRONT = (-N_META) % CHUNK
LANES = 128
SUBLANES = 8
ROW_BLOCK = 320
INPROJ_ROWS = 416
VMEM_LIMIT = 56 * 1024 * 1024


def _silu(x):
    return x * (1.0 / (1.0 + jnp.exp(-x)))


def _softplus(x):
    return jnp.maximum(x, 0.0) + jnp.log(1.0 + jnp.exp(-jnp.abs(x)))


def _const_spec(shape):
    nd = len(shape)
    return pl.BlockSpec(shape, lambda *_: (0,) * nd, pipeline_mode=pl.Buffered(1))


def _row_spec(rows, cols):
    return pl.BlockSpec((None, rows, cols), lambda b, j: (b, j, 0))


def _inproj_body(x_ref, ng_ref, wqkv_ref, wz_ref, wab_ref, wb_ref, wc_ref, wh_ref, wzb_ref,
                 cw_ref, scw_ref, alog_ref, dtb_ref,
                 q_ref, k_ref, v_ref, sz_ref, osc_ref, gb_ref,
                 qkv_scr, sc_scr, conv_scr, *, tb):
    j = pl.program_id(1)
    tbp = tb // SUBLANES
    wide = 2 * LANES

    @pl.when(j == 0)
    def _():
        qkv_scr[:, 0:SUBLANES, :] = jnp.zeros((3 * HEADS, SUBLANES, LANES), F32)
        sc_scr[:, 0:SUBLANES, :] = jnp.zeros((HEADS, SUBLANES, LANES), F32)

    x = x_ref[...]
    ms = jnp.mean(x * x, axis=-1, keepdims=True)
    hb = (x * lax.rsqrt(ms + EPS) * ng_ref[...]).astype(BF16)

    def proj(w_ref, n):
        return jnp.dot(hb, w_ref[:, n * wide:(n + 1) * wide], preferred_element_type=F32)

    def strided(ref, slab, start):
        return ref[slab, pl.ds(start, SUBLANES, stride=tbp), :]

    for n in range(3 * D_MODEL // wide):
        r = proj(wqkv_ref, n)
        qkv_scr[2 * n, SUBLANES:SUBLANES + tb, :] = r[:, :LANES]
        qkv_scr[2 * n + 1, SUBLANES:SUBLANES + tb, :] = r[:, LANES:]
    first = SUBLANES - (DN_CONV - 1)
    outs = (q_ref, k_ref, v_ref)
    for s in range(3 * HEADS):
        which, hh = divmod(s, HEADS)
        sl = slice(s * LANES, (s + 1) * LANES)
        w = [jnp.broadcast_to(cw_ref[t:t + 1, sl], (SUBLANES, LANES)) for t in range(DN_CONV)]
        for v in range(tbp):
            y = w[0] * strided(qkv_scr, s, first + v)
            for t in range(1, DN_CONV):
                y = y + w[t] * strided(qkv_scr, s, first + v + t)
            y = _silu(y)
            if which < 2:
                y = y * lax.rsqrt(jnp.sum(y * y, axis=-1, keepdims=True) + EPS)
                if which == 0:
                    y = y * (HEAD_DIM ** -0.5)
            outs[which][hh, pl.ds(v, SUBLANES, stride=tbp), :] = y
    qkv_scr[:, first:SUBLANES, :] = qkv_scr[:, first + tb:SUBLANES + tb, :]

    for n in range(D_MODEL // wide):
        sz_ref[:, n * wide:(n + 1) * wide] = _silu(proj(wz_ref, n))

    pab = jnp.dot(hb, wab_ref[...], preferred_element_type=F32)
    g = -jnp.exp(alog_ref[...]) * _softplus(pab + dtb_ref[...])
    beta = 1.0 / (1.0 + jnp.exp(-pab))
    lane = lax.broadcasted_iota(jnp.int32, (tb, LANES), 1)
    row = lax.broadcasted_iota(jnp.int32, (tb, LANES), 0) + j * tb
    gb = jnp.where(lane < HEADS, g, beta)
    gb_ref[...] = jnp.where(row >= PAD_FRONT, gb, 0.0)

    for n in range(D_MODEL // wide):
        u = proj(wc_ref, n) * proj(wh_ref, n)
        sc_scr[2 * n, SUBLANES:SUBLANES + tb, :] = u[:, :LANES]
        sc_scr[2 * n + 1, SUBLANES:SUBLANES + tb, :] = u[:, LANES:]
    first = SUBLANES - (SC_CONV - 1)
    for s in range(D_MODEL // LANES):
        sl = slice(s * LANES, (s + 1) * LANES)
        w = [jnp.broadcast_to(scw_ref[t:t + 1, sl], (SUBLANES, LANES)) for t in range(SC_CONV)]
        for v in range(tbp):
            y = w[0] * strided(sc_scr, s, first + v)
            for t in range(1, SC_CONV):
                y = y + w[t] * strided(sc_scr, s, first + v + t)
            conv_scr[s, pl.ds(v, SUBLANES, stride=tbp), :] = y
    sc_scr[:, first:SUBLANES, :] = sc_scr[:, first + tb:SUBLANES + tb, :]
    for n in range(D_MODEL // wide):
        conv = jnp.concatenate([conv_scr[2 * n], conv_scr[2 * n + 1]], axis=1)
        y = proj(wb_ref, n) * conv * _silu(proj(wzb_ref, n))
        osc_ref[:, n * wide:(n + 1) * wide] = y.astype(BF16)


def _inproj(xp, ng, wqkv, wz, wab, wb, wc, wh, wzb, cw, scw, alog, dtb):
    bsz, lp, _ = xp.shape
    tb = INPROJ_ROWS
    row = _row_spec(tb, D_MODEL)
    heads = pl.BlockSpec((None, HEADS, tb, HEAD_DIM), lambda b, j: (b, 0, j, 0))
    per_head = jax.ShapeDtypeStruct((bsz, HEADS, lp, HEAD_DIM), F32)
    return pl.pallas_call(
        functools.partial(_inproj_body, tb=tb),
        grid=(bsz, lp // tb),
        in_specs=[row] + [_const_spec(a.shape) for a in
                          (ng, wqkv, wz, wab, wb, wc, wh, wzb, cw, scw, alog, dtb)],
        out_specs=[heads, heads, heads, row, row, _row_spec(tb, LANES)],
        out_shape=[per_head, per_head, per_head,
                   jax.ShapeDtypeStruct((bsz, lp, D_MODEL), F32),
                   jax.ShapeDtypeStruct((bsz, lp, D_MODEL), BF16),
                   jax.ShapeDtypeStruct((bsz, lp, LANES), F32)],
        scratch_shapes=[pltpu.VMEM((3 * HEADS, SUBLANES + tb, LANES), F32),
                        pltpu.VMEM((HEADS, SUBLANES + tb, LANES), F32),
                        pltpu.VMEM((HEADS, tb, LANES), F32)],
        compiler_params=pltpu.CompilerParams(
            dimension_semantics=("arbitrary", "arbitrary"), vmem_limit_bytes=VMEM_LIMIT),
        name="inproj",
    )(xp, ng, wqkv, wz, wab, wb, wc, wh, wzb, cw, scw, alog, dtb)


def _mm(a, b):
    return jnp.dot(a.astype(BF16), b.astype(BF16), preferred_element_type=F32)


def _chunk_body(q_ref, k_ref, v_ref, gb_ref, c_ref, g_ref, qt_ref, oin_ref, gam_ref, *, nchunk):
    ri = lax.broadcasted_iota(jnp.int32, (CHUNK, CHUNK), 0)
    ci = lax.broadcasted_iota(jnp.int32, (CHUNK, CHUNK), 1)
    causal = ri >= ci
    strict = ri > ci
    ltri = causal.astype(F32)

    probs = [(c, h) for c in range(nchunk) for h in range(HEADS)]
    rows = [slice(c * CHUNK, (c + 1) * CHUNK) for c in range(nchunk)]
    sls = [slice(h * HEAD_DIM, (h + 1) * HEAD_DIM) for h in range(HEADS)]
    gbc, gcum, gcum_t, e_g, e_kd = [], [], [], [], []
    for c in range(nchunk):
        gbc.append(gb_ref[rows[c], :])
        gcum.append(jnp.dot(ltri, gbc[c], precision=lax.Precision.HIGHEST,
                            preferred_element_type=F32))
        gcum_t.append(gcum[c].T)
        e_g.append(jnp.exp(gcum[c]))
        e_kd.append(jnp.exp(gcum[c][CHUNK - 1:CHUNK, :] - gcum[c]))
        gam_ref[c] = jnp.broadcast_to(jnp.exp(gcum_t[c][0:HEADS, CHUNK - 1:CHUNK]), (HEADS, LANES))
    qh = [q_ref[h, rows[c], :] for c, h in probs]
    kh = [k_ref[h, rows[c], :] for c, h in probs]
    bcol = [gbc[c][:, HEADS + h:HEADS + h + 1] for c, h in probs]
    egc = [e_g[c][:, h:h + 1] for c, h in probs]
    dec = [jnp.where(causal, jnp.exp(gcum[c][:, h:h + 1] - gcum_t[c][h:h + 1, :]), 0.0)
           for c, h in probs]
    ps = range(len(probs))
    qkk = [lax.dot_general(jnp.concatenate([qh[p], kh[p]], axis=0).astype(BF16),
                           kh[p].astype(BF16), (((1,), (1,)), ((), ())),
                           preferred_element_type=F32) for p in ps]
    xm = [jnp.where(strict, -(bcol[p] * qkk[p][CHUNK:] * dec[p]), 0.0) for p in ps]
    pw = [_mm(xm[p], xm[p]) for p in ps]
    for _ in range(4):
        both = [_mm(jnp.concatenate([xm[p], pw[p]], axis=0), pw[p]) for p in ps]
        xm = [xm[p] + pw[p] + both[p][:CHUNK] for p in ps]
        pw = [both[p][CHUNK:] for p in ps]
    last = [_mm(xm[p], pw[p]) for p in ps]
    xm = [xm[p] + pw[p] + last[p] for p in ps]
    rhs = [jnp.concatenate([bcol[p] * v_ref[h, rows[c], :], (bcol[p] * egc[p]) * kh[p]], axis=1)
           for p, (c, h) in enumerate(probs)]
    xr = [_mm(xm[p], rhs[p]) for p in ps]
    uw = [(rhs[p] + xr[p]).astype(BF16) for p in ps]
    cg = [lax.dot_general((kh[p] * e_kd[c][:, h:h + 1]).astype(BF16), uw[p],
                          (((0,), (0,)), ((), ())), preferred_element_type=F32)
          for p, (c, h) in enumerate(probs)]
    op = [jnp.dot((qkk[p][:CHUNK] * dec[p]).astype(BF16), uw[p], preferred_element_type=F32)
          for p in ps]
    for p, (c, h) in enumerate(probs):
        c_ref[c, sls[h], :] = cg[p][:, :HEAD_DIM]
        g_ref[c, sls[h], :] = cg[p][:, HEAD_DIM:].astype(BF16)
        oin_ref[rows[c], sls[h]] = op[p][:, :HEAD_DIM]
        qt_ref[rows[c], sls[h]] = (qh[p] * egc[p] - op[p][:, HEAD_DIM:]).astype(BF16)


def _chunk_call(q, k, v, gb):
    bsz, _, lp, _ = q.shape
    tb = ROW_BLOCK
    nchunk = tb // CHUNK
    nc_total = lp // CHUNK
    row = _row_spec(tb, D_MODEL)
    heads = pl.BlockSpec((None, HEADS, tb, HEAD_DIM), lambda b, j: (b, 0, j, 0))
    mat = pl.BlockSpec((None, nchunk, D_MODEL, HEAD_DIM), lambda b, j: (b, j, 0, 0))
    gam = pl.BlockSpec((None, nchunk, SUBLANES, LANES), lambda b, j: (b, j, 0, 0))
    return pl.pallas_call(
        functools.partial(_chunk_body, nchunk=nchunk),
        grid=(bsz, lp // tb),
        in_specs=[heads, heads, heads, _row_spec(tb, LANES)],
        out_specs=[mat, mat, row, row, gam],
        out_shape=[jax.ShapeDtypeStruct((bsz, nc_total, D_MODEL, HEAD_DIM), F32),
                   jax.ShapeDtypeStruct((bsz, nc_total, D_MODEL, HEAD_DIM), BF16),
                   jax.ShapeDtypeStruct((bsz, lp, D_MODEL), BF16),
                   jax.ShapeDtypeStruct((bsz, lp, D_MODEL), F32),
                   jax.ShapeDtypeStruct((bsz, nc_total, SUBLANES, LANES), F32)],
        compiler_params=pltpu.CompilerParams(
            dimension_semantics=("arbitrary", "arbitrary"), vmem_limit_bytes=VMEM_LIMIT),
        name="chunk",
    )(q, k, v, gb)


def _scanout_body(x_ref, qt_ref, oin_ref, c_ref, g_ref, gam_ref, sz_ref, osc_ref, og_ref, wout_ref,
                  fg_ref, y_ref, s_scr, *, final, nchunk, row_block):
    j = pl.program_id(1)

    @pl.when(j == 0)
    def _():
        s_scr[...] = jnp.zeros(s_scr.shape, F32)

    hs = range(HEADS)
    sls = [slice(h * HEAD_DIM, (h + 1) * HEAD_DIM) for h in hs]
    s = [s_scr[h] for h in hs]
    o = [[] for _ in hs]
    for c in range(nchunk):
        rows = slice(c * CHUNK, (c + 1) * CHUNK)
        gam = gam_ref[c]
        r = [jnp.dot(jnp.concatenate([g_ref[c, sls[h], :], qt_ref[rows, sls[h]]], axis=0),
                     s[h].astype(BF16), preferred_element_type=F32) for h in hs]
        for h in hs:
            o[h].append(oin_ref[rows, sls[h]] + r[h][HEAD_DIM:])
            s[h] = gam[h:h + 1, :] * s[h] + c_ref[c, sls[h], :] - r[h][:HEAD_DIM]
    for h in hs:
        s_scr[h] = s[h]

    parts = []
    for h in hs:
        oh = jnp.concatenate(o[h], axis=0)
        on = oh * lax.rsqrt(jnp.mean(oh * oh, axis=-1, keepdims=True) + EPS) * og_ref[...]
        parts.append((on * sz_ref[:, sls[h]]).astype(BF16))
    parts.append(osc_ref[...])
    mix = jnp.concatenate(parts, axis=1)
    y = x_ref[...] + jnp.dot(mix, wout_ref[...], preferred_element_type=F32)
    row = lax.broadcasted_iota(jnp.int32, y.shape, 0) + j * row_block
    y = jnp.where(row >= PAD_FRONT, y, 0.0)
    if final:
        y = y * lax.rsqrt(jnp.mean(y * y, axis=-1, keepdims=True) + EPS) * fg_ref[...]
    y_ref[...] = y


def _scanout(xp, qt, oin, cm, gm, gam, sz, osc, og, wout, fg, final):
    bsz, lp, _ = xp.shape
    tb = ROW_BLOCK
    nchunk = tb // CHUNK
    row = _row_spec(tb, D_MODEL)
    mat = pl.BlockSpec((None, nchunk, D_MODEL, HEAD_DIM), lambda b, j: (b, j, 0, 0))
    gsp = pl.BlockSpec((None, nchunk, SUBLANES, LANES), lambda b, j: (b, j, 0, 0))
    return pl.pallas_call(
        functools.partial(_scanout_body, final=final, nchunk=nchunk, row_block=tb),
        grid=(bsz, lp // tb),
        in_specs=[row, row, row, mat, mat, gsp, row, row, _const_spec(og.shape),
                  _const_spec(wout.shape), _const_spec(fg.shape)],
        out_specs=row,
        out_shape=jax.ShapeDtypeStruct((bsz, lp, D_MODEL), F32),
        scratch_shapes=[pltpu.VMEM((HEADS, HEAD_DIM, HEAD_DIM), F32)],
        compiler_params=pltpu.CompilerParams(
            dimension_semantics=("arbitrary", "arbitrary"), vmem_limit_bytes=VMEM_LIMIT),
        name="scanout_final" if final else "scanout",
    )(xp, qt, oin, cm, gm, gam, sz, osc, og, wout, fg)


def _pad_lanes(v, fill=0.0):
    return jnp.pad(v.astype(F32), (0, LANES - v.shape[0]), constant_values=fill)[None, :]


def kernel(x, meta_tokens, norm_g, w_in, dn_conv_w, dn_A_log, dn_dt_bias, dn_out_g, sc_conv_w,
           w_out, final_g):
    bsz, seq, d = x.shape
    depth = w_in.shape[0]
    dn = HEADS * HEAD_DIM
    sc = (w_in.shape[2] - 4 * dn - 2 * HEADS) // 4
    assert d == D_MODEL and sc == D_MODEL and dn == D_MODEL
    lp = PAD_FRONT + N_META + seq
    assert lp % ROW_BLOCK == 0 and lp % INPROJ_ROWS == 0

    meta = jnp.broadcast_to(meta_tokens[None].astype(x.dtype), (bsz, N_META, d))
    xp = jnp.concatenate([jnp.zeros((bsz, PAD_FRONT, d), x.dtype), meta, x], axis=1)

    o0 = 4 * dn + 2 * HEADS
    for l in range(depth):
        w = w_in[l]
        wqkv = w[:, :3 * dn].astype(BF16)
        wz = w[:, 3 * dn:4 * dn].astype(BF16)
        wab = jnp.pad(w[:, 4 * dn:o0], ((0, 0), (0, LANES - 2 * HEADS))).astype(BF16)
        wb = w[:, o0:o0 + sc].astype(BF16)
        wc = w[:, o0 + sc:o0 + 2 * sc].astype(BF16)
        wh = w[:, o0 + 2 * sc:o0 + 3 * sc].astype(BF16)
        wzb = w[:, o0 + 3 * sc:o0 + 4 * sc].astype(BF16)
        q, k, v, sz, osc, gb = _inproj(
            xp, norm_g[l][None, :], wqkv, wz, wab, wb, wc, wh, wzb,
            dn_conv_w[l], sc_conv_w[l], _pad_lanes(dn_A_log[l]), _pad_lanes(dn_dt_bias[l]))
        cm, gm, qt, oin, gam = _chunk_call(q, k, v, gb)
        xp = _scanout(xp, qt, oin, cm, gm, gam, sz, osc, dn_out_g[l][None, :],
                      w_out[l].astype(BF16), final_g[None, :], final=(l == depth - 1))
    return xp[:, PAD_FRONT + N_META:]
```

```python
import functools

import jax
import jax.numpy as jnp
from jax import lax
from jax.experimental import pallas as pl
from jax.experimental.pallas import tpu as pltpu

F32 = jnp.float32
BF16 = jnp.bfloat16

D_MODEL = 1024
N_META = 16
HEADS = 8
HEAD_DIM = 128
DN_CONV = 4
SC_CONV = 3
CHUNK = 64
EPS = 1e-6
PAD_FRONT = (-N_META) % CHUNK
LANES = 128
SUBLANES = 8
ROW_BLOCK = 320
INPROJ_ROWS = 416
VMEM_LIMIT = 56 * 1024 * 1024


def _silu(x):
    return x * (1.0 / (1.0 + jnp.exp(-x)))


def _softplus(x):
    return jnp.maximum(x, 0.0) + jnp.log(1.0 + jnp.exp(-jnp.abs(x)))


def _const_spec(shape):
    nd = len(shape)
    return pl.BlockSpec(shape, lambda *_: (0,) * nd, pipeline_mode=pl.Buffered(1))


def _row_spec(rows, cols):
    return pl.BlockSpec((None, rows, cols), lambda b, j: (b, j, 0))


def _inproj_body(x_ref, ng_ref, wqkv_ref, wz_ref, wab_ref, wb_ref, wc_ref, wh_ref, wzb_ref,
                 cw_ref, scw_ref, alog_ref, dtb_ref,
                 q_ref, k_ref, v_ref, sz_ref, osc_ref, gb_ref,
                 qkv_scr, sc_scr, conv_scr, *, tb):
    j = pl.program_id(1)
    tbp = tb // SUBLANES
    wide = 2 * LANES

    @pl.when(j == 0)
    def _():
        qkv_scr[:, 0:SUBLANES, :] = jnp.zeros((3 * HEADS, SUBLANES, LANES), F32)
        sc_scr[:, 0:SUBLANES, :] = jnp.zeros((HEADS, SUBLANES, LANES), F32)

    x = x_ref[...]
    ms = jnp.mean(x * x, axis=-1, keepdims=True)
    hb = (x * lax.rsqrt(ms + EPS) * ng_ref[...]).astype(BF16)

    def proj(w_ref, n):
        return jnp.dot(hb, w_ref[:, n * wide:(n + 1) * wide], preferred_element_type=F32)

    def strided(ref, slab, start):
        return ref[slab, pl.ds(start, SUBLANES, stride=tbp), :]

    for n in range(3 * D_MODEL // wide):
        r = proj(wqkv_ref, n)
        qkv_scr[2 * n, SUBLANES:SUBLANES + tb, :] = r[:, :LANES]
        qkv_scr[2 * n + 1, SUBLANES:SUBLANES + tb, :] = r[:, LANES:]
    first = SUBLANES - (DN_CONV - 1)
    outs = (q_ref, k_ref, v_ref)
    for s in range(3 * HEADS):
        which, hh = divmod(s, HEADS)
        sl = slice(s * LANES, (s + 1) * LANES)
        w = [jnp.broadcast_to(cw_ref[t:t + 1, sl], (SUBLANES, LANES)) for t in range(DN_CONV)]
        for v in range(tbp):
            y = w[0] * strided(qkv_scr, s, first + v)
            for t in range(1, DN_CONV):
                y = y + w[t] * strided(qkv_scr, s, first + v + t)
            y = _silu(y)
            if which < 2:
                y = y * lax.rsqrt(jnp.sum(y * y, axis=-1, keepdims=True) + EPS)
                if which == 0:
                    y = y * (HEAD_DIM ** -0.5)
            outs[which][hh, pl.ds(v, SUBLANES, stride=tbp), :] = y
    qkv_scr[:, first:SUBLANES, :] = qkv_scr[:, first + tb:SUBLANES + tb, :]

    for n in range(D_MODEL // wide):
        sz_ref[:, n * wide:(n + 1) * wide] = _silu(proj(wz_ref, n))

    pab = jnp.dot(hb, wab_ref[...], preferred_element_type=F32)
    g = -jnp.exp(alog_ref[...]) * _softplus(pab + dtb_ref[...])
    beta = 1.0 / (1.0 + jnp.exp(-pab))
    lane = lax.broadcasted_iota(jnp.int32, (tb, LANES), 1)
    row = lax.broadcasted_iota(jnp.int32, (tb, LANES), 0) + j * tb
    gb = jnp.where(lane < HEADS, g, beta)
    gb_ref[...] = jnp.where(row >= PAD_FRONT, gb, 0.0)

    for n in range(D_MODEL // wide):
        u = proj(wc_ref, n) * proj(wh_ref, n)
        sc_scr[2 * n, SUBLANES:SUBLANES + tb, :] = u[:, :LANES]
        sc_scr[2 * n + 1, SUBLANES:SUBLANES + tb, :] = u[:, LANES:]
    first = SUBLANES - (SC_CONV - 1)
    for s in range(D_MODEL // LANES):
        sl = slice(s * LANES, (s + 1) * LANES)
        w = [jnp.broadcast_to(scw_ref[t:t + 1, sl], (SUBLANES, LANES)) for t in range(SC_CONV)]
        for v in range(tbp):
            y = w[0] * strided(sc_scr, s, first + v)
            for t in range(1, SC_CONV):
                y = y + w[t] * strided(sc_scr, s, first + v + t)
            conv_scr[s, pl.ds(v, SUBLANES, stride=tbp), :] = y
    sc_scr[:, first:SUBLANES, :] = sc_scr[:, first + tb:SUBLANES + tb, :]
    for n in range(D_MODEL // wide):
        conv = jnp.concatenate([conv_scr[2 * n], conv_scr[2 * n + 1]], axis=1)
        y = proj(wb_ref, n) * conv * _silu(proj(wzb_ref, n))
        osc_ref[:, n * wide:(n + 1) * wide] = y.astype(BF16)


def _inproj(xp, ng, wqkv, wz, wab, wb, wc, wh, wzb, cw, scw, alog, dtb):
    bsz, lp, _ = xp.shape
    tb = INPROJ_ROWS
    row = _row_spec(tb, D_MODEL)
    heads = pl.BlockSpec((None, HEADS, tb, HEAD_DIM), lambda b, j: (b, 0, j, 0))
    per_head = jax.ShapeDtypeStruct((bsz, HEADS, lp, HEAD_DIM), F32)
    return pl.pallas_call(
        functools.partial(_inproj_body, tb=tb),
        grid=(bsz, lp // tb),
        in_specs=[row] + [_const_spec(a.shape) for a in
                          (ng, wqkv, wz, wab, wb, wc, wh, wzb, cw, scw, alog, dtb)],
        out_specs=[heads, heads, heads, row, row, _row_spec(tb, LANES)],
        out_shape=[per_head, per_head, per_head,
                   jax.ShapeDtypeStruct((bsz, lp, D_MODEL), F32),
                   jax.ShapeDtypeStruct((bsz, lp, D_MODEL), BF16),
                   jax.ShapeDtypeStruct((bsz, lp, LANES), F32)],
        scratch_shapes=[pltpu.VMEM((3 * HEADS, SUBLANES + tb, LANES), F32),
                        pltpu.VMEM((HEADS, SUBLANES + tb, LANES), F32),
                        pltpu.VMEM((HEADS, tb, LANES), F32)],
        compiler_params=pltpu.CompilerParams(
            dimension_semantics=("arbitrary", "arbitrary"), vmem_limit_bytes=VMEM_LIMIT),
        name="inproj",
    )(xp, ng, wqkv, wz, wab, wb, wc, wh, wzb, cw, scw, alog, dtb)


def _chunk_terms(q_ref, k_ref, v_ref, gb_ref, nchunk):
    ri = lax.broadcasted_iota(jnp.int32, (CHUNK, LANES), 0)
    li = lax.broadcasted_iota(jnp.int32, (CHUNK, LANES), 1)
    left = li < CHUNK
    lj = jnp.where(left, li, li - CHUNK)
    causal = ri >= lj
    strict = ri > lj
    r64 = lax.broadcasted_iota(jnp.int32, (CHUNK, CHUNK), 0)
    c64 = lax.broadcasted_iota(jnp.int32, (CHUNK, CHUNK), 1)
    ltri = (r64 >= c64).astype(F32)
    zero_k = jnp.zeros((CHUNK, HEAD_DIM), BF16)

    def pair_cols(x, a, b):
        return jnp.where(left, x[:, a:a + 1], x[:, b:b + 1])

    def block_diag(p):
        return jnp.concatenate([jnp.where(left, p, 0.0), jnp.where(left, 0.0, p)],
                               axis=0).astype(BF16)

    rows = [slice(c * CHUNK, (c + 1) * CHUNK) for c in range(nchunk)]
    gbc, gcum, grow, e_g, e_kd, gam = [], [], [], [], [], []
    for c in range(nchunk):
        gbc.append(gb_ref[rows[c], :])
        gcum.append(jnp.dot(ltri, gbc[c], precision=lax.Precision.HIGHEST,
                            preferred_element_type=F32))
        grow.append(jnp.concatenate([gcum[c], gcum[c]], axis=0).T[0:HEADS, :])
        e_g.append(jnp.exp(gcum[c]))
        e_kd.append(jnp.exp(gcum[c][CHUNK - 1:CHUNK, :] - gcum[c]))
        gam.append(jnp.broadcast_to(jnp.exp(grow[c][:, CHUNK - 1:CHUNK]), (HEADS, LANES)))

    probs = [(c, h) for c in range(nchunk) for h in range(HEADS)]
    pairs = [(c, p) for c in range(nchunk) for p in range(HEADS // 2)]
    ps = range(len(probs))
    qs = range(len(pairs))
    qh = [q_ref[h, rows[c], :] for c, h in probs]
    kh = [k_ref[h, rows[c], :] for c, h in probs]
    bcol = [gbc[c][:, HEADS + h:HEADS + h + 1] for c, h in probs]
    egc = [e_g[c][:, h:h + 1] for c, h in probs]
    dec = [jnp.where(causal, jnp.exp(pair_cols(gcum[c], 2 * p, 2 * p + 1)
                                     - jnp.where(left, grow[c][2 * p:2 * p + 1, :],
                                                 grow[c][2 * p + 1:2 * p + 2, :])), 0.0)
           for c, p in pairs]
    beta2 = [pair_cols(gbc[c], HEADS + 2 * p, HEADS + 2 * p + 1) for c, p in pairs]
    qkk = []
    for i in qs:
        a, b = 2 * i, 2 * i + 1
        lhs = jnp.concatenate([jnp.concatenate([qh[a], qh[b]], axis=1),
                               jnp.concatenate([kh[a], kh[b]], axis=1)], axis=0).astype(BF16)
        ka, kb = kh[a].astype(BF16), kh[b].astype(BF16)
        rhs_t = jnp.concatenate([jnp.concatenate([ka, zero_k], axis=1),
                                 jnp.concatenate([zero_k, kb], axis=1)], axis=0)
        qkk.append(lax.dot_general(lhs, rhs_t, (((1,), (1,)), ((), ())),
                                   preferred_element_type=F32))
    xm = [jnp.where(strict, -(beta2[i] * qkk[i][CHUNK:] * dec[i]), 0.0) for i in qs]
    pw = [jnp.dot(xm[i].astype(BF16), block_diag(xm[i]), preferred_element_type=F32) for i in qs]
    for _ in range(4):
        both = [jnp.dot(jnp.concatenate([xm[i], pw[i]], axis=0).astype(BF16), block_diag(pw[i]),
                        preferred_element_type=F32) for i in qs]
        xm = [xm[i] + pw[i] + both[i][:CHUNK] for i in qs]
        pw = [both[i][CHUNK:] for i in qs]
    last = [jnp.dot(xm[i].astype(BF16), block_diag(pw[i]), preferred_element_type=F32) for i in qs]
    xmb = [(xm[i] + pw[i] + last[i]).astype(BF16) for i in qs]
    pmb = [(qkk[i][:CHUNK] * dec[i]).astype(BF16) for i in qs]

    def pad_rows(m, p):
        z = jnp.zeros_like(m)
        return jnp.concatenate([m, z] if p % 2 == 0 else [z, m], axis=0)

    rhs = [jnp.concatenate([bcol[p] * v_ref[h, rows[c], :], (bcol[p] * egc[p]) * kh[p]], axis=1)
           for p, (c, h) in enumerate(probs)]
    xr = [jnp.dot(xmb[p // 2], pad_rows(rhs[p].astype(BF16), p), preferred_element_type=F32)
          for p in ps]
    uw = [(rhs[p] + xr[p]).astype(BF16) for p in ps]
    cg = [lax.dot_general((kh[p] * e_kd[c][:, h:h + 1]).astype(BF16), uw[p],
                          (((0,), (0,)), ((), ())), preferred_element_type=F32)
          for p, (c, h) in enumerate(probs)]
    op = [jnp.dot(pmb[p // 2], pad_rows(uw[p], p), preferred_element_type=F32) for p in ps]
    cm = [cg[p][:, :HEAD_DIM] for p in ps]
    gm = [cg[p][:, HEAD_DIM:].astype(BF16) for p in ps]
    oin = [op[p][:, :HEAD_DIM] for p in ps]
    qt = [(qh[p] * egc[p] - op[p][:, HEAD_DIM:]).astype(BF16) for p in ps]
    return cm, gm, qt, oin, gam


def _delta_body(x_ref, q_ref, k_ref, v_ref, gb_ref, sz_ref, osc_ref, og_ref, wout_ref, fg_ref,
                y_ref, s_scr, *, final, nchunk, row_block):
    j = pl.program_id(1)

    @pl.when(j == 0)
    def _():
        s_scr[...] = jnp.zeros(s_scr.shape, F32)

    cm, gm, qt, oin, gam = _chunk_terms(q_ref, k_ref, v_ref, gb_ref, nchunk)

    hs = range(HEADS)
    sls = [slice(h * HEAD_DIM, (h + 1) * HEAD_DIM) for h in hs]
    s = [s_scr[h] for h in hs]
    o = [[] for _ in hs]
    for c in range(nchunk):
        ps = [c * HEADS + h for h in hs]
        r = [jnp.dot(jnp.concatenate([gm[ps[h]], qt[ps[h]]], axis=0), s[h].astype(BF16),
                     preferred_element_type=F32) for h in hs]
        for h in hs:
            o[h].append(oin[ps[h]] + r[h][HEAD_DIM:])
            s[h] = gam[c][h:h + 1, :] * s[h] + cm[ps[h]] - r[h][:HEAD_DIM]
    for h in hs:
        s_scr[h] = s[h]

    parts = []
    for h in hs:
        oh = jnp.concatenate(o[h], axis=0)
        on = oh * lax.rsqrt(jnp.mean(oh * oh, axis=-1, keepdims=True) + EPS) * og_ref[...]
        parts.append((on * sz_ref[:, sls[h]]).astype(BF16))
    parts.append(osc_ref[...])
    mix = jnp.concatenate(parts, axis=1)
    y = x_ref[...] + jnp.dot(mix, wout_ref[...], preferred_element_type=F32)
    row = lax.broadcasted_iota(jnp.int32, y.shape, 0) + j * row_block
    y = jnp.where(row >= PAD_FRONT, y, 0.0)
    if final:
        y = y * lax.rsqrt(jnp.mean(y * y, axis=-1, keepdims=True) + EPS) * fg_ref[...]
    y_ref[...] = y


def _delta(xp, q, k, v, gb, sz, osc, og, wout, fg, final):
    bsz, lp, _ = xp.shape
    tb = ROW_BLOCK
    row = _row_spec(tb, D_MODEL)
    heads = pl.BlockSpec((None, HEADS, tb, HEAD_DIM), lambda b, j: (b, 0, j, 0))
    return pl.pallas_call(
        functools.partial(_delta_body, final=final, nchunk=tb // CHUNK, row_block=tb),
        grid=(bsz, lp // tb),
        in_specs=[row, heads, heads, heads, _row_spec(tb, LANES), row, row,
                  _const_spec(og.shape), _const_spec(wout.shape), _const_spec(fg.shape)],
        out_specs=row,
        out_shape=jax.ShapeDtypeStruct((bsz, lp, D_MODEL), F32),
        scratch_shapes=[pltpu.VMEM((HEADS, HEAD_DIM, HEAD_DIM), F32)],
        compiler_params=pltpu.CompilerParams(
            dimension_semantics=("arbitrary", "arbitrary"), vmem_limit_bytes=VMEM_LIMIT),
        name="delta_final" if final else "delta",
    )(xp, q, k, v, gb, sz, osc, og, wout, fg)


def _pad_lanes(v, fill=0.0):
    return jnp.pad(v.astype(F32), (0, LANES - v.shape[0]), constant_values=fill)[None, :]


def kernel(x, meta_tokens, norm_g, w_in, dn_conv_w, dn_A_log, dn_dt_bias, dn_out_g, sc_conv_w,
           w_out, final_g):
    bsz, seq, d = x.shape
    depth = w_in.shape[0]
    dn = HEADS * HEAD_DIM
    sc = (w_in.shape[2] - 4 * dn - 2 * HEADS) // 4
    assert d == D_MODEL and sc == D_MODEL and dn == D_MODEL
    lp = PAD_FRONT + N_META + seq
    assert lp % ROW_BLOCK == 0 and lp % INPROJ_ROWS == 0

    meta = jnp.broadcast_to(meta_tokens[None].astype(x.dtype), (bsz, N_META, d))
    xp = jnp.concatenate([jnp.zeros((bsz, PAD_FRONT, d), x.dtype), meta, x], axis=1)

    o0 = 4 * dn + 2 * HEADS
    for l in range(depth):
        w = w_in[l]
        wqkv = w[:, :3 * dn].astype(BF16)
        wz = w[:, 3 * dn:4 * dn].astype(BF16)
        wab = jnp.pad(w[:, 4 * dn:o0], ((0, 0), (0, LANES - 2 * HEADS))).astype(BF16)
        wb = w[:, o0:o0 + sc].astype(BF16)
        wc = w[:, o0 + sc:o0 + 2 * sc].astype(BF16)
        wh = w[:, o0 + 2 * sc:o0 + 3 * sc].astype(BF16)
        wzb = w[:, o0 + 3 * sc:o0 + 4 * sc].astype(BF16)
        q, k, v, sz, osc, gb = _inproj(
            xp, norm_g[l][None, :], wqkv, wz, wab, wb, wc, wh, wzb,
            dn_conv_w[l], sc_conv_w[l], _pad_lanes(dn_A_log[l]), _pad_lanes(dn_dt_bias[l]))
        xp = _delta(xp, q, k, v, gb, sz, osc, dn_out_g[l][None, :], w_out[l].astype(BF16),
                    final_g[None, :], final=(l == depth - 1))
    return xp[:, PAD_FRONT + N_META:]
```

```python
import functools

import jax
import jax.numpy as jnp
from jax import lax
from jax.experimental import pallas as pl
from jax.experimental.pallas import tpu as pltpu

F32 = jnp.float32
BF16 = jnp.bfloat16

D_MODEL = 1024
N_META = 16
HEADS = 8
HEAD_DIM = 128
DN_CONV = 4
SC_CONV = 3
CHUNK = 64
EPS = 1e-6
PAD_FRONT = (-N_META) % CHUNK
LANES = 128
SUBLANES = 8
ROW_BLOCK = 320
INPROJ_ROWS = 416
VMEM_LIMIT = 56 * 1024 * 1024


def _silu(x):
    return x * (1.0 / (1.0 + jnp.exp(-x)))


def _softplus(x):
    return jnp.maximum(x, 0.0) + jnp.log(1.0 + jnp.exp(-jnp.abs(x)))


def _const_spec(shape):
    nd = len(shape)
    return pl.BlockSpec(shape, lambda *_: (0,) * nd, pipeline_mode=pl.Buffered(1))


def _row_spec(rows, cols):
    return pl.BlockSpec((None, rows, cols), lambda b, j: (b, j, 0))


def _inproj_body(x_ref, ng_ref, wqkv_ref, wz_ref, wab_ref, wb_ref, wc_ref, wh_ref, wzb_ref,
                 cw_ref, scw_ref, alog_ref, dtb_ref,
                 q_ref, k_ref, v_ref, sz_ref, osc_ref, gb_ref,
                 qkv_scr, sc_scr, conv_scr, *, tb):
    j = pl.program_id(1)
    tbp = tb // SUBLANES
    wide = 2 * LANES

    @pl.when(j == 0)
    def _():
        qkv_scr[:, 0:SUBLANES, :] = jnp.zeros((3 * HEADS, SUBLANES, LANES), F32)
        sc_scr[:, 0:SUBLANES, :] = jnp.zeros((HEADS, SUBLANES, LANES), F32)

    x = x_ref[...]
    ms = jnp.mean(x * x, axis=-1, keepdims=True)
    hb = (x * lax.rsqrt(ms + EPS) * ng_ref[...]).astype(BF16)

    def proj(w_ref, n):
        return jnp.dot(hb, w_ref[:, n * wide:(n + 1) * wide], preferred_element_type=F32)

    def strided(ref, slab, start):
        return ref[slab, pl.ds(start, SUBLANES, stride=tbp), :]

    for n in range(3 * D_MODEL // wide):
        r = proj(wqkv_ref, n)
        qkv_scr[2 * n, SUBLANES:SUBLANES + tb, :] = r[:, :LANES]
        qkv_scr[2 * n + 1, SUBLANES:SUBLANES + tb, :] = r[:, LANES:]
    first = SUBLANES - (DN_CONV - 1)
    outs = (q_ref, k_ref, v_ref)
    for s in range(3 * HEADS):
        which, hh = divmod(s, HEADS)
        sl = slice(s * LANES, (s + 1) * LANES)
        w = [jnp.broadcast_to(cw_ref[t:t + 1, sl], (SUBLANES, LANES)) for t in range(DN_CONV)]
        for v in range(tbp):
            y = w[0] * strided(qkv_scr, s, first + v)
            for t in range(1, DN_CONV):
                y = y + w[t] * strided(qkv_scr, s, first + v + t)
            y = _silu(y)
            if which < 2:
                y = y * lax.rsqrt(jnp.sum(y * y, axis=-1, keepdims=True) + EPS)
                if which == 0:
                    y = y * (HEAD_DIM ** -0.5)
            outs[which][hh, pl.ds(v, SUBLANES, stride=tbp), :] = y
    qkv_scr[:, first:SUBLANES, :] = qkv_scr[:, first + tb:SUBLANES + tb, :]

    for n in range(D_MODEL // wide):
        sz_ref[:, n * wide:(n + 1) * wide] = _silu(proj(wz_ref, n))

    pab = jnp.dot(hb, wab_ref[...], preferred_element_type=F32)
    g = -jnp.exp(alog_ref[...]) * _softplus(pab + dtb_ref[...])
    beta = 1.0 / (1.0 + jnp.exp(-pab))
    lane = lax.broadcasted_iota(jnp.int32, (tb, LANES), 1)
    row = lax.broadcasted_iota(jnp.int32, (tb, LANES), 0) + j * tb
    gb = jnp.where(lane < HEADS, g, beta)
    gb_ref[...] = jnp.where(row >= PAD_FRONT, gb, 0.0)

    for n in range(D_MODEL // wide):
        u = proj(wc_ref, n) * proj(wh_ref, n)
        sc_scr[2 * n, SUBLANES:SUBLANES + tb, :] = u[:, :LANES]
        sc_scr[2 * n + 1, SUBLANES:SUBLANES + tb, :] = u[:, LANES:]
    first = SUBLANES - (SC_CONV - 1)
    for s in range(D_MODEL // LANES):
        sl = slice(s * LANES, (s + 1) * LANES)
        w = [jnp.broadcast_to(scw_ref[t:t + 1, sl], (SUBLANES, LANES)) for t in range(SC_CONV)]
        for v in range(tbp):
            y = w[0] * strided(sc_scr, s, first + v)
            for t in range(1, SC_CONV):
                y = y + w[t] * strided(sc_scr, s, first + v + t)
            conv_scr[s, pl.ds(v, SUBLANES, stride=tbp), :] = y
    sc_scr[:, first:SUBLANES, :] = sc_scr[:, first + tb:SUBLANES + tb, :]
    for n in range(D_MODEL // wide):
        conv = jnp.concatenate([conv_scr[2 * n], conv_scr[2 * n + 1]], axis=1)
        y = proj(wb_ref, n) * conv * _silu(proj(wzb_ref, n))
        osc_ref[:, n * wide:(n + 1) * wide] = y.astype(BF16)


def _inproj(xp, ng, wqkv, wz, wab, wb, wc, wh, wzb, cw, scw, alog, dtb):
    bsz, lp, _ = xp.shape
    tb = INPROJ_ROWS
    row = _row_spec(tb, D_MODEL)
    heads = pl.BlockSpec((None, HEADS, tb, HEAD_DIM), lambda b, j: (b, 0, j, 0))
    per_head = jax.ShapeDtypeStruct((bsz, HEADS, lp, HEAD_DIM), F32)
    return pl.pallas_call(
        functools.partial(_inproj_body, tb=tb),
        grid=(bsz, lp // tb),
        in_specs=[row] + [_const_spec(a.shape) for a in
                          (ng, wqkv, wz, wab, wb, wc, wh, wzb, cw, scw, alog, dtb)],
        out_specs=[heads, heads, heads, row, row, _row_spec(tb, LANES)],
        out_shape=[per_head, per_head, per_head,
                   jax.ShapeDtypeStruct((bsz, lp, D_MODEL), F32),
                   jax.ShapeDtypeStruct((bsz, lp, D_MODEL), BF16),
                   jax.ShapeDtypeStruct((bsz, lp, LANES), F32)],
        scratch_shapes=[pltpu.VMEM((3 * HEADS, SUBLANES + tb, LANES), F32),
                        pltpu.VMEM((HEADS, SUBLANES + tb, LANES), F32),
                        pltpu.VMEM((HEADS, tb, LANES), F32)],
        compiler_params=pltpu.CompilerParams(
            dimension_semantics=("arbitrary", "arbitrary"), vmem_limit_bytes=VMEM_LIMIT),
        name="inproj",
    )(xp, ng, wqkv, wz, wab, wb, wc, wh, wzb, cw, scw, alog, dtb)


def _pair_masks():
    ri = lax.broadcasted_iota(jnp.int32, (CHUNK, LANES), 0)
    li = lax.broadcasted_iota(jnp.int32, (CHUNK, LANES), 1)
    left = li < CHUNK
    lj = jnp.where(left, li, li - CHUNK)
    return left, ri >= lj, ri > lj


def _pair_cols(left, x, a, b):
    return jnp.where(left, x[:, a:a + 1], x[:, b:b + 1])


def _decay_terms(gb_ref, nchunk):
    left, causal, _ = _pair_masks()
    r64 = lax.broadcasted_iota(jnp.int32, (CHUNK, CHUNK), 0)
    c64 = lax.broadcasted_iota(jnp.int32, (CHUNK, CHUNK), 1)
    ltri = (r64 >= c64).astype(BF16)
    e_g, e_kd, gam, dec = [], [], [], []
    for c in range(nchunk):
        g = gb_ref[c * CHUNK:(c + 1) * CHUNK, :]
        hi = g.astype(BF16)
        rest = g - hi.astype(F32)
        mid = rest.astype(BF16)
        lo = (rest - mid.astype(F32)).astype(BF16)
        gcum = (jnp.dot(ltri, hi, preferred_element_type=F32)
                + jnp.dot(ltri, mid, preferred_element_type=F32)
                + jnp.dot(ltri, lo, preferred_element_type=F32))
        grow = jnp.concatenate([gcum, gcum], axis=0).T[0:HEADS, :]
        e_g.append(jnp.exp(gcum))
        e_kd.append(jnp.exp(gcum[CHUNK - 1:CHUNK, :] - gcum))
        gam.append(jnp.broadcast_to(jnp.exp(grow[:, CHUNK - 1:CHUNK]), (HEADS, LANES)))
        for p in range(HEADS // 2):
            diff = (_pair_cols(left, gcum, 2 * p, 2 * p + 1)
                    - jnp.where(left, grow[2 * p:2 * p + 1, :], grow[2 * p + 1:2 * p + 2, :]))
            dec.append(jnp.where(causal, jnp.exp(diff), 0.0))
    return e_g, e_kd, gam, dec


def _chunk_terms(q_ref, k_ref, v_ref, gb_ref, decay, nchunk, hooks=()):
    hooks = list(hooks)

    def stage_done():
        if hooks:
            for fn in hooks.pop(0):
                fn()

    e_g, e_kd, _, dec = decay
    left, _, strict = _pair_masks()
    zero_k = jnp.zeros((CHUNK, HEAD_DIM), BF16)

    def block_diag(p):
        return jnp.concatenate([jnp.where(left, p, 0.0), jnp.where(left, 0.0, p)],
                               axis=0).astype(BF16)

    rows = [slice(c * CHUNK, (c + 1) * CHUNK) for c in range(nchunk)]
    gbc = [gb_ref[rows[c], :] for c in range(nchunk)]

    probs = [(c, h) for c in range(nchunk) for h in range(HEADS)]
    pairs = [(c, p) for c in range(nchunk) for p in range(HEADS // 2)]
    ps = range(len(probs))
    qs = range(len(pairs))
    qh = [q_ref[h, rows[c], :] for c, h in probs]
    kh = [k_ref[h, rows[c], :] for c, h in probs]
    bcol = [gbc[c][:, HEADS + h:HEADS + h + 1] for c, h in probs]
    egc = [e_g[c][:, h:h + 1] for c, h in probs]
    beta2 = [_pair_cols(left, gbc[c], HEADS + 2 * p, HEADS + 2 * p + 1) for c, p in pairs]
    qkk = []
    for i in qs:
        a, b = 2 * i, 2 * i + 1
        lhs = jnp.concatenate([jnp.concatenate([qh[a], qh[b]], axis=1),
                               jnp.concatenate([kh[a], kh[b]], axis=1)], axis=0).astype(BF16)
        ka, kb = kh[a].astype(BF16), kh[b].astype(BF16)
        rhs_t = jnp.concatenate([jnp.concatenate([ka, zero_k], axis=1),
                                 jnp.concatenate([zero_k, kb], axis=1)], axis=0)
        qkk.append(lax.dot_general(lhs, rhs_t, (((1,), (1,)), ((), ())),
                                   preferred_element_type=F32))
    stage_done()
    xm = [jnp.where(strict, -(beta2[i] * qkk[i][CHUNK:] * dec[i]), 0.0) for i in qs]
    pw = [jnp.dot(xm[i].astype(BF16), block_diag(xm[i]), preferred_element_type=F32) for i in qs]
    stage_done()
    for _ in range(4):
        both = [jnp.dot(jnp.concatenate([xm[i], pw[i]], axis=0).astype(BF16), block_diag(pw[i]),
                        preferred_element_type=F32) for i in qs]
        stage_done()
        xm = [xm[i] + pw[i] + both[i][:CHUNK] for i in qs]
        pw = [both[i][CHUNK:] for i in qs]
    last = [jnp.dot(xm[i].astype(BF16), block_diag(pw[i]), preferred_element_type=F32) for i in qs]
    stage_done()
    xmb = [(xm[i] + pw[i] + last[i]).astype(BF16) for i in qs]
    pmb = [(qkk[i][:CHUNK] * dec[i]).astype(BF16) for i in qs]

    def pad_rows(m, p):
        z = jnp.zeros_like(m)
        return jnp.concatenate([m, z] if p % 2 == 0 else [z, m], axis=0)

    rhs = [jnp.concatenate([bcol[p] * v_ref[h, rows[c], :], (bcol[p] * egc[p]) * kh[p]], axis=1)
           for p, (c, h) in enumerate(probs)]
    xr = [jnp.dot(xmb[p // 2], pad_rows(rhs[p].astype(BF16), p), preferred_element_type=F32)
          for p in ps]
    stage_done()
    uw = [(rhs[p] + xr[p]).astype(BF16) for p in ps]
    cg = [lax.dot_general((kh[p] * e_kd[c][:, h:h + 1]).astype(BF16), uw[p],
                          (((0,), (0,)), ((), ())), preferred_element_type=F32)
          for p, (c, h) in enumerate(probs)]
    stage_done()
    op = [jnp.dot(pmb[p // 2], pad_rows(uw[p], p), preferred_element_type=F32) for p in ps]
    stage_done()
    while hooks:
        stage_done()
    cm = [cg[p][:, :HEAD_DIM] for p in ps]
    gm = [cg[p][:, HEAD_DIM:].astype(BF16) for p in ps]
    oin = [op[p][:, :HEAD_DIM] for p in ps]
    qt = [(qh[p] * egc[p] - op[p][:, HEAD_DIM:]).astype(BF16) for p in ps]
    return cm, gm, qt, oin


def _delta_body(x_ref, q_ref, k_ref, v_ref, gb_ref, sz_ref, osc_ref, og_ref, wout_ref, fg_ref,
                y_ref, s_scr, cm_scr, gm_scr, qt_scr, oin_scr, gam_scr,
                *, final, nchunk, nblk, row_block):
    t = pl.program_id(0)
    cur = lax.rem(t, 2)
    prev = 1 - cur
    jprev = lax.rem(jnp.maximum(t - 1, 0), nblk)

    @pl.when(t == 0)
    def _():
        s_scr[...] = jnp.zeros(s_scr.shape, F32)
        cm_scr[1] = jnp.zeros(cm_scr.shape[1:], F32)
        gm_scr[1] = jnp.zeros(gm_scr.shape[1:], BF16)
        qt_scr[1] = jnp.zeros(qt_scr.shape[1:], BF16)
        oin_scr[1] = jnp.zeros(oin_scr.shape[1:], F32)
        gam_scr[1] = jnp.zeros(gam_scr.shape[1:], F32)

    hs = range(HEADS)
    sls = [slice(h * HEAD_DIM, (h + 1) * HEAD_DIM) for h in hs]
    first_block = jprev == 0
    s = [jnp.where(first_block, 0.0, s_scr[h]) for h in hs]
    o = [[] for _ in hs]
    mix = []
    pieces = []
    wide = 2 * LANES

    def scan_step(c):
        def run():
            ps = [c * HEADS + h for h in hs]
            gam = gam_scr[prev, c]
            r = [jnp.dot(jnp.concatenate([gm_scr[prev, ps[h]], qt_scr[prev, ps[h]]], axis=0),
                         s[h].astype(BF16), preferred_element_type=F32) for h in hs]
            for h in hs:
                o[h].append(oin_scr[prev, ps[h]] + r[h][HEAD_DIM:])
                s[h] = gam[h:h + 1, :] * s[h] + cm_scr[prev, ps[h]] - r[h][:HEAD_DIM]
        return run

    def gate():
        for h in hs:
            s_scr[h] = s[h]
        parts = []
        for h in hs:
            oh = jnp.concatenate(o[h], axis=0)
            on = oh * lax.rsqrt(jnp.mean(oh * oh, axis=-1, keepdims=True) + EPS) * og_ref[...]
            parts.append((on * sz_ref[:, sls[h]]).astype(BF16))
        parts.append(osc_ref[...])
        mix.append(jnp.concatenate(parts, axis=1))

    def out_piece(n):
        def run():
            cols = slice(n * wide, (n + 1) * wide)
            y = x_ref[:, cols] + jnp.dot(mix[0], wout_ref[:, cols], preferred_element_type=F32)
            row = lax.broadcasted_iota(jnp.int32, y.shape, 0) + jprev * row_block
            y = jnp.where(row >= PAD_FRONT, y, 0.0)
            if final:
                pieces.append(y)
            else:
                y_ref[:, cols] = y
        return run

    steps = [scan_step(c) for c in range(nchunk)]
    outs = [out_piece(n) for n in range(D_MODEL // wide)]
    hooks = [steps[:3], steps[3:4], steps[4:], [gate]] + [[fn] for fn in outs]
    decay = _decay_terms(gb_ref, nchunk)
    cm, gm, qt, oin = _chunk_terms(q_ref, k_ref, v_ref, gb_ref, decay, nchunk, hooks)
    if final:
        y = jnp.concatenate(pieces, axis=1)
        y_ref[...] = y * lax.rsqrt(jnp.mean(y * y, axis=-1, keepdims=True) + EPS) * fg_ref[...]

    for p in range(nchunk * HEADS):
        cm_scr[cur, p] = cm[p]
        gm_scr[cur, p] = gm[p]
        qt_scr[cur, p] = qt[p]
        oin_scr[cur, p] = oin[p]
    for c in range(nchunk):
        gam_scr[cur, c] = decay[2][c]


def _delta(xp, q, k, v, gb, sz, osc, og, wout, fg, final):
    bsz, lp, _ = xp.shape
    tb = ROW_BLOCK
    nchunk = tb // CHUNK
    nblk = lp // tb
    total = bsz * nblk

    def cur_idx(t):
        tc = jnp.minimum(t, total - 1)
        return tc // nblk, tc % nblk

    def prev_idx(t):
        tp = jnp.maximum(t - 1, 0)
        return tp // nblk, tp % nblk

    def cur_heads(t):
        b, j = cur_idx(t)
        return b, 0, j, 0

    def cur_rows(t):
        b, j = cur_idx(t)
        return b, j, 0

    def prev_rows(t):
        b, j = prev_idx(t)
        return b, j, 0

    heads = pl.BlockSpec((None, HEADS, tb, HEAD_DIM), cur_heads)
    prow = pl.BlockSpec((None, tb, D_MODEL), prev_rows)
    nprob = nchunk * HEADS
    return pl.pallas_call(
        functools.partial(_delta_body, final=final, nchunk=nchunk, nblk=nblk, row_block=tb),
        grid=(total + 1,),
        in_specs=[prow, heads, heads, heads, pl.BlockSpec((None, tb, LANES), cur_rows), prow, prow,
                  _const_spec(og.shape), _const_spec(wout.shape), _const_spec(fg.shape)],
        out_specs=prow,
        out_shape=jax.ShapeDtypeStruct((bsz, lp, D_MODEL), F32),
        scratch_shapes=[pltpu.VMEM((HEADS, HEAD_DIM, HEAD_DIM), F32),
                        pltpu.VMEM((2, nprob, HEAD_DIM, HEAD_DIM), F32),
                        pltpu.VMEM((2, nprob, HEAD_DIM, HEAD_DIM), BF16),
                        pltpu.VMEM((2, nprob, CHUNK, HEAD_DIM), BF16),
                        pltpu.VMEM((2, nprob, CHUNK, HEAD_DIM), F32),
                        pltpu.VMEM((2, nchunk, HEADS, LANES), F32)],
        compiler_params=pltpu.CompilerParams(
            dimension_semantics=("arbitrary",), vmem_limit_bytes=VMEM_LIMIT),
        name="delta_final" if final else "delta",
    )(xp, q, k, v, gb, sz, osc, og, wout, fg)


def _pad_lanes(v, fill=0.0):
    return jnp.pad(v.astype(F32), (0, LANES - v.shape[0]), constant_values=fill)[None, :]


def kernel(x, meta_tokens, norm_g, w_in, dn_conv_w, dn_A_log, dn_dt_bias, dn_out_g, sc_conv_w,
           w_out, final_g):
    bsz, seq, d = x.shape
    depth = w_in.shape[0]
    dn = HEADS * HEAD_DIM
    sc = (w_in.shape[2] - 4 * dn - 2 * HEADS) // 4
    assert d == D_MODEL and sc == D_MODEL and dn == D_MODEL
    lp = PAD_FRONT + N_META + seq
    assert lp % ROW_BLOCK == 0 and lp % INPROJ_ROWS == 0

    meta = jnp.broadcast_to(meta_tokens[None].astype(x.dtype), (bsz, N_META, d))
    xp = jnp.concatenate([jnp.zeros((bsz, PAD_FRONT, d), x.dtype), meta, x], axis=1)

    o0 = 4 * dn + 2 * HEADS
    for l in range(depth):
        w = w_in[l]
        wqkv = w[:, :3 * dn].astype(BF16)
        wz = w[:, 3 * dn:4 * dn].astype(BF16)
        wab = jnp.pad(w[:, 4 * dn:o0], ((0, 0), (0, LANES - 2 * HEADS))).astype(BF16)
        wb = w[:, o0:o0 + sc].astype(BF16)
        wc = w[:, o0 + sc:o0 + 2 * sc].astype(BF16)
        wh = w[:, o0 + 2 * sc:o0 + 3 * sc].astype(BF16)
        wzb = w[:, o0 + 3 * sc:o0 + 4 * sc].astype(BF16)
        q, k, v, sz, osc, gb = _inproj(
            xp, norm_g[l][None, :], wqkv, wz, wab, wb, wc, wh, wzb,
            dn_conv_w[l], sc_conv_w[l], _pad_lanes(dn_A_log[l]), _pad_lanes(dn_dt_bias[l]))
        xp = _delta(xp, q, k, v, gb, sz, osc, dn_out_g[l][None, :], w_out[l].astype(BF16),
                    final_g[None, :], final=(l == depth - 1))
    return xp[:, PAD_FRONT + N_META:]
```

```python
import functools

import jax
import jax.numpy as jnp
from jax import lax
from jax.experimental import pallas as pl
from jax.experimental.pallas import tpu as pltpu

F32 = jnp.float32
BF16 = jnp.bfloat16

D_MODEL = 1024
N_META = 16
HEADS = 8
HEAD_DIM = 128
DN_CONV = 4
SC_CONV = 3
CHUNK = 64
EPS = 1e-6
PAD_FRONT = (-N_META) % CHUNK
LANES = 128
SUBLANES = 8
ROW_BLOCK = 320
INPROJ_ROWS = 416
VMEM_LIMIT = 56 * 1024 * 1024


def _silu(x):
    h = 0.5 * x
    return h + h * jnp.tanh(h)


def _softplus(x):
    return jnp.maximum(x, 0.0) + jnp.log(1.0 + jnp.exp(-jnp.abs(x)))


def _const_spec(shape):
    nd = len(shape)
    return pl.BlockSpec(shape, lambda *_: (0,) * nd, pipeline_mode=pl.Buffered(1))


def _row_spec(rows, cols):
    return pl.BlockSpec((None, rows, cols), lambda b, j: (b, j, 0))


def _first_layer_rows(xprev_ref, xcur_ref, meta_ref, j, rows, cols=slice(None)):
    front = PAD_FRONT + N_META
    header = jnp.concatenate([jnp.zeros((PAD_FRONT, D_MODEL), F32), meta_ref[...]], axis=0)[:, cols]
    head = jnp.where(j == 0, header, xprev_ref[rows - front:rows, cols])
    return jnp.concatenate([head, xcur_ref[0:rows - front, cols]], axis=0)


def _inproj_body(*refs, tb, first):
    if first:
        xprev_ref, xcur_ref, meta_ref = refs[:3]
        refs = refs[3:]
    else:
        x_ref = refs[0]
        refs = refs[1:]
    (ng_ref, wqkv_ref, wz_ref, wab_ref, wb_ref, wc_ref, wh_ref, wzb_ref,
     cw_ref, scw_ref, alog_ref, dtb_ref,
     q_ref, k_ref, v_ref, sz_ref, osc_ref, gb_ref,
     qkv_scr, sc_scr, conv_scr) = refs
    j = pl.program_id(1)
    tbp = tb // SUBLANES
    wide = 2 * LANES

    @pl.when(j == 0)
    def _():
        qkv_scr[:, 0:SUBLANES, :] = jnp.zeros((3 * HEADS, SUBLANES, LANES), F32)
        sc_scr[:, 0:SUBLANES, :] = jnp.zeros((HEADS, SUBLANES, LANES), F32)

    x = _first_layer_rows(xprev_ref, xcur_ref, meta_ref, j, tb) if first else x_ref[...]
    ms = jnp.mean(x * x, axis=-1, keepdims=True)
    hb = (x * lax.rsqrt(ms + EPS) * ng_ref[...]).astype(BF16)

    def proj(w_ref, n):
        return jnp.dot(hb, w_ref[:, n * wide:(n + 1) * wide], preferred_element_type=F32)

    def strided(ref, slab, start):
        return ref[slab, pl.ds(start, SUBLANES, stride=tbp), :]

    for n in range(3 * D_MODEL // wide):
        r = proj(wqkv_ref, n)
        qkv_scr[2 * n, SUBLANES:SUBLANES + tb, :] = r[:, :LANES]
        qkv_scr[2 * n + 1, SUBLANES:SUBLANES + tb, :] = r[:, LANES:]
    first = SUBLANES - (DN_CONV - 1)
    outs = (q_ref, k_ref, v_ref)
    for s in range(3 * HEADS):
        which, hh = divmod(s, HEADS)
        sl = slice(s * LANES, (s + 1) * LANES)
        w = [jnp.broadcast_to(cw_ref[t:t + 1, sl], (SUBLANES, LANES)) for t in range(DN_CONV)]
        for v in range(tbp):
            y = w[0] * strided(qkv_scr, s, first + v)
            for t in range(1, DN_CONV):
                y = y + w[t] * strided(qkv_scr, s, first + v + t)
            y = _silu(y)
            if which < 2:
                inv = lax.rsqrt(jnp.sum(y * y, axis=-1, keepdims=True) + EPS)
                y = y * (inv * (HEAD_DIM ** -0.5) if which == 0 else inv)
            outs[which][hh, pl.ds(v, SUBLANES, stride=tbp), :] = y
    qkv_scr[:, first:SUBLANES, :] = qkv_scr[:, first + tb:SUBLANES + tb, :]

    for n in range(D_MODEL // wide):
        sz_ref[:, n * wide:(n + 1) * wide] = _silu(proj(wz_ref, n))

    pab = jnp.dot(hb, wab_ref[...], preferred_element_type=F32)
    g = -jnp.exp(alog_ref[...]) * _softplus(pab + dtb_ref[...])
    beta = 1.0 / (1.0 + jnp.exp(-pab))
    lane = lax.broadcasted_iota(jnp.int32, (tb, LANES), 1)
    row = lax.broadcasted_iota(jnp.int32, (tb, LANES), 0) + j * tb
    gb = jnp.where(lane < HEADS, g, beta)
    gb_ref[...] = jnp.where(row >= PAD_FRONT, gb, 0.0)

    for n in range(D_MODEL // wide):
        u = proj(wc_ref, n) * proj(wh_ref, n)
        sc_scr[2 * n, SUBLANES:SUBLANES + tb, :] = u[:, :LANES]
        sc_scr[2 * n + 1, SUBLANES:SUBLANES + tb, :] = u[:, LANES:]
    first = SUBLANES - (SC_CONV - 1)
    for s in range(D_MODEL // LANES):
        sl = slice(s * LANES, (s + 1) * LANES)
        w = [jnp.broadcast_to(scw_ref[t:t + 1, sl], (SUBLANES, LANES)) for t in range(SC_CONV)]
        for v in range(tbp):
            y = w[0] * strided(sc_scr, s, first + v)
            for t in range(1, SC_CONV):
                y = y + w[t] * strided(sc_scr, s, first + v + t)
            conv_scr[s, pl.ds(v, SUBLANES, stride=tbp), :] = y
    sc_scr[:, first:SUBLANES, :] = sc_scr[:, first + tb:SUBLANES + tb, :]
    for n in range(D_MODEL // wide):
        conv = jnp.concatenate([conv_scr[2 * n], conv_scr[2 * n + 1]], axis=1)
        y = proj(wb_ref, n) * conv * _silu(proj(wzb_ref, n))
        osc_ref[:, n * wide:(n + 1) * wide] = y.astype(BF16)


def _inproj(xs, lp, ng, wqkv, wz, wab, wb, wc, wh, wzb, cw, scw, alog, dtb):
    first = len(xs) == 3
    bsz = xs[0].shape[0]
    tb = INPROJ_ROWS
    row = _row_spec(tb, D_MODEL)
    if first:
        x_specs = [pl.BlockSpec((None, tb, D_MODEL), lambda b, j: (b, jnp.maximum(j - 1, 0), 0)),
                   row, _const_spec(xs[2].shape)]
    else:
        x_specs = [row]
    heads = pl.BlockSpec((None, HEADS, tb, HEAD_DIM), lambda b, j: (b, 0, j, 0))
    per_head = jax.ShapeDtypeStruct((bsz, HEADS, lp, HEAD_DIM), F32)
    return pl.pallas_call(
        functools.partial(_inproj_body, tb=tb, first=first),
        grid=(bsz, lp // tb),
        in_specs=x_specs + [_const_spec(a.shape) for a in
                          (ng, wqkv, wz, wab, wb, wc, wh, wzb, cw, scw, alog, dtb)],
        out_specs=[heads, heads, heads, row, row, _row_spec(tb, LANES)],
        out_shape=[per_head, per_head, per_head,
                   jax.ShapeDtypeStruct((bsz, lp, D_MODEL), F32),
                   jax.ShapeDtypeStruct((bsz, lp, D_MODEL), BF16),
                   jax.ShapeDtypeStruct((bsz, lp, LANES), F32)],
        scratch_shapes=[pltpu.VMEM((3 * HEADS, SUBLANES + tb, LANES), F32),
                        pltpu.VMEM((HEADS, SUBLANES + tb, LANES), F32),
                        pltpu.VMEM((HEADS, tb, LANES), F32)],
        compiler_params=pltpu.CompilerParams(
            dimension_semantics=("arbitrary", "arbitrary"), vmem_limit_bytes=VMEM_LIMIT),
        name="inproj_first" if first else "inproj",
    )(*xs, ng, wqkv, wz, wab, wb, wc, wh, wzb, cw, scw, alog, dtb)


def _pair_masks():
    ri = lax.broadcasted_iota(jnp.int32, (CHUNK, LANES), 0)
    li = lax.broadcasted_iota(jnp.int32, (CHUNK, LANES), 1)
    left = li < CHUNK
    lj = jnp.where(left, li, li - CHUNK)
    return left, ri >= lj, ri > lj


def _pair_cols(left, x, a, b):
    return jnp.where(left, x[:, a:a + 1], x[:, b:b + 1])


def _decay_terms(gb_ref, nchunk):
    left, causal, _ = _pair_masks()
    r64 = lax.broadcasted_iota(jnp.int32, (CHUNK, CHUNK), 0)
    c64 = lax.broadcasted_iota(jnp.int32, (CHUNK, CHUNK), 1)
    ltri = (r64 >= c64).astype(BF16)
    e_g, e_kd, gam, dec = [], [], [], []
    for c in range(nchunk):
        g = gb_ref[c * CHUNK:(c + 1) * CHUNK, :]
        hi = g.astype(BF16)
        rest = g - hi.astype(F32)
        mid = rest.astype(BF16)
        lo = (rest - mid.astype(F32)).astype(BF16)
        gcum = (jnp.dot(ltri, hi, preferred_element_type=F32)
                + jnp.dot(ltri, mid, preferred_element_type=F32)
                + jnp.dot(ltri, lo, preferred_element_type=F32))
        grow = jnp.concatenate([gcum, gcum], axis=0).T[0:HEADS, :]
        e_g.append(jnp.exp(gcum))
        e_kd.append(jnp.exp(gcum[CHUNK - 1:CHUNK, :] - gcum))
        gam.append(jnp.broadcast_to(jnp.exp(grow[:, CHUNK - 1:CHUNK]), (HEADS, LANES)))
        for p in range(HEADS // 2):
            diff = (_pair_cols(left, gcum, 2 * p, 2 * p + 1)
                    - jnp.where(left, grow[2 * p:2 * p + 1, :], grow[2 * p + 1:2 * p + 2, :]))
            dec.append(jnp.where(causal, jnp.exp(diff), 0.0))
    return e_g, e_kd, gam, dec


def _chunk_terms(q_ref, k_ref, v_ref, gb_ref, decay, nchunk, hooks=()):
    hooks = list(hooks)

    def stage_done():
        if hooks:
            for fn in hooks.pop(0):
                fn()

    e_g, e_kd, _, dec = decay
    left, _, strict = _pair_masks()
    zero_k = jnp.zeros((CHUNK, HEAD_DIM), BF16)

    def block_diag(p):
        return jnp.concatenate([jnp.where(left, p, 0.0), jnp.where(left, 0.0, p)],
                               axis=0).astype(BF16)

    rows = [slice(c * CHUNK, (c + 1) * CHUNK) for c in range(nchunk)]
    gbc = [gb_ref[rows[c], :] for c in range(nchunk)]

    probs = [(c, h) for c in range(nchunk) for h in range(HEADS)]
    pairs = [(c, p) for c in range(nchunk) for p in range(HEADS // 2)]
    ps = range(len(probs))
    qs = range(len(pairs))
    qh = [q_ref[h, rows[c], :] for c, h in probs]
    kh = [k_ref[h, rows[c], :] for c, h in probs]
    bcol = [gbc[c][:, HEADS + h:HEADS + h + 1] for c, h in probs]
    egc = [e_g[c][:, h:h + 1] for c, h in probs]
    beta2 = [_pair_cols(left, gbc[c], HEADS + 2 * p, HEADS + 2 * p + 1) for c, p in pairs]
    qkk = []
    for i in qs:
        a, b = 2 * i, 2 * i + 1
        lhs = jnp.concatenate([jnp.concatenate([qh[a], qh[b]], axis=1),
                               jnp.concatenate([kh[a], kh[b]], axis=1)], axis=0).astype(BF16)
        ka, kb = kh[a].astype(BF16), kh[b].astype(BF16)
        rhs_t = jnp.concatenate([jnp.concatenate([ka, zero_k], axis=1),
                                 jnp.concatenate([zero_k, kb], axis=1)], axis=0)
        qkk.append(lax.dot_general(lhs, rhs_t, (((1,), (1,)), ((), ())),
                                   preferred_element_type=F32))
    stage_done()
    xm = [jnp.where(strict, -(beta2[i] * qkk[i][CHUNK:] * dec[i]), 0.0) for i in qs]
    pw = [jnp.dot(xm[i].astype(BF16), block_diag(xm[i]), preferred_element_type=F32) for i in qs]
    stage_done()
    for _ in range(4):
        both = [jnp.dot(jnp.concatenate([xm[i], pw[i]], axis=0).astype(BF16), block_diag(pw[i]),
                        preferred_element_type=F32) for i in qs]
        stage_done()
        xm = [xm[i] + pw[i] + both[i][:CHUNK] for i in qs]
        pw = [both[i][CHUNK:] for i in qs]
    last = [jnp.dot(xm[i].astype(BF16), block_diag(pw[i]), preferred_element_type=F32) for i in qs]
    stage_done()
    xmb = [(xm[i] + pw[i] + last[i]).astype(BF16) for i in qs]
    pmb = [(qkk[i][:CHUNK] * dec[i]).astype(BF16) for i in qs]

    def pad_rows(m, p):
        z = jnp.zeros_like(m)
        return jnp.concatenate([m, z] if p % 2 == 0 else [z, m], axis=0)

    rhs = [jnp.concatenate([bcol[p] * v_ref[h, rows[c], :], (bcol[p] * egc[p]) * kh[p]], axis=1)
           for p, (c, h) in enumerate(probs)]
    xr = [jnp.dot(xmb[p // 2], pad_rows(rhs[p].astype(BF16), p), preferred_element_type=F32)
          for p in ps]
    stage_done()
    uw = [(rhs[p] + xr[p]).astype(BF16) for p in ps]
    cg = [lax.dot_general((kh[p] * e_kd[c][:, h:h + 1]).astype(BF16), uw[p],
                          (((0,), (0,)), ((), ())), preferred_element_type=F32)
          for p, (c, h) in enumerate(probs)]
    stage_done()
    op = [jnp.dot(pmb[p // 2], pad_rows(uw[p], p), preferred_element_type=F32) for p in ps]
    stage_done()
    while hooks:
        stage_done()
    cm = [cg[p][:, :HEAD_DIM] for p in ps]
    gm = [cg[p][:, HEAD_DIM:].astype(BF16) for p in ps]
    oin = [op[p][:, :HEAD_DIM] for p in ps]
    qt = [(qh[p] * egc[p] - op[p][:, HEAD_DIM:]).astype(BF16) for p in ps]
    return cm, gm, qt, oin


def _delta_body(*refs, first, final, nchunk, nblk, row_block):
    if first:
        xprev_ref, xcur_ref, meta_ref = refs[:3]
        refs = refs[3:]
    else:
        x_ref = refs[0]
        refs = refs[1:]
    (q_ref, k_ref, v_ref, gb_ref, sz_ref, osc_ref, og_ref, wout_ref, fg_ref,
     y_ref, s_scr, cm_scr, gm_scr, qt_scr, oin_scr, gam_scr) = refs
    t = pl.program_id(0)
    cur = lax.rem(t, 2)
    prev = 1 - cur
    jprev = lax.rem(jnp.maximum(t - 1, 0), nblk)

    @pl.when(t == 0)
    def _():
        s_scr[...] = jnp.zeros(s_scr.shape, F32)
        cm_scr[1] = jnp.zeros(cm_scr.shape[1:], F32)
        gm_scr[1] = jnp.zeros(gm_scr.shape[1:], BF16)
        qt_scr[1] = jnp.zeros(qt_scr.shape[1:], BF16)
        oin_scr[1] = jnp.zeros(oin_scr.shape[1:], F32)
        gam_scr[1] = jnp.zeros(gam_scr.shape[1:], F32)

    hs = range(HEADS)
    sls = [slice(h * HEAD_DIM, (h + 1) * HEAD_DIM) for h in hs]
    first_block = jprev == 0
    s = [jnp.where(first_block, 0.0, s_scr[h]) for h in hs]
    o = [[] for _ in hs]
    mix = []
    pieces = []
    wide = 2 * LANES

    def scan_step(c):
        def run():
            ps = [c * HEADS + h for h in hs]
            gam = gam_scr[prev, c]
            r = [jnp.dot(jnp.concatenate([gm_scr[prev, ps[h]], qt_scr[prev, ps[h]]], axis=0),
                         s[h].astype(BF16), preferred_element_type=F32) for h in hs]
            for h in hs:
                o[h].append(oin_scr[prev, ps[h]] + r[h][HEAD_DIM:])
                s[h] = gam[h:h + 1, :] * s[h] + cm_scr[prev, ps[h]] - r[h][:HEAD_DIM]
        return run

    def gate():
        for h in hs:
            s_scr[h] = s[h]
        parts = []
        for h in hs:
            oh = jnp.concatenate(o[h], axis=0)
            on = oh * lax.rsqrt(jnp.mean(oh * oh, axis=-1, keepdims=True) + EPS) * og_ref[...]
            parts.append((on * sz_ref[:, sls[h]]).astype(BF16))
        parts.append(osc_ref[...])
        mix.append(jnp.concatenate(parts, axis=1))

    def out_piece(n):
        def run():
            cols = slice(n * wide, (n + 1) * wide)
            if first:
                res = _first_layer_rows(xprev_ref, xcur_ref, meta_ref, jprev, row_block, cols)
            else:
                res = x_ref[:, cols]
            y = res + jnp.dot(mix[0], wout_ref[:, cols], preferred_element_type=F32)
            row = lax.broadcasted_iota(jnp.int32, y.shape, 0) + jprev * row_block
            y = jnp.where(row >= PAD_FRONT, y, 0.0)
            if final:
                pieces.append(y)
            else:
                y_ref[:, cols] = y
        return run

    steps = [scan_step(c) for c in range(nchunk)]
    outs = [out_piece(n) for n in range(D_MODEL // wide)]
    hooks = [steps[:3], steps[3:4], steps[4:], [gate]] + [[fn] for fn in outs]
    decay = _decay_terms(gb_ref, nchunk)
    cm, gm, qt, oin = _chunk_terms(q_ref, k_ref, v_ref, gb_ref, decay, nchunk, hooks)
    if final:
        y = jnp.concatenate(pieces, axis=1)
        y_ref[...] = y * lax.rsqrt(jnp.mean(y * y, axis=-1, keepdims=True) + EPS) * fg_ref[...]

    for p in range(nchunk * HEADS):
        cm_scr[cur, p] = cm[p]
        gm_scr[cur, p] = gm[p]
        qt_scr[cur, p] = qt[p]
        oin_scr[cur, p] = oin[p]
    for c in range(nchunk):
        gam_scr[cur, c] = decay[2][c]


def _delta(xs, q, k, v, gb, sz, osc, og, wout, fg, final):
    first = len(xs) == 3
    bsz, _, lp, _ = q.shape
    tb = ROW_BLOCK
    nchunk = tb // CHUNK
    nblk = lp // tb
    total = bsz * nblk

    def cur_idx(t):
        tc = jnp.minimum(t, total - 1)
        return tc // nblk, tc % nblk

    def prev_idx(t):
        tp = jnp.maximum(t - 1, 0)
        return tp // nblk, tp % nblk

    def cur_heads(t):
        b, j = cur_idx(t)
        return b, 0, j, 0

    def cur_rows(t):
        b, j = cur_idx(t)
        return b, j, 0

    def prev_rows(t):
        b, j = prev_idx(t)
        return b, j, 0

    def prev_rows_before(t):
        b, j = prev_idx(t)
        return b, jnp.maximum(j - 1, 0), 0

    heads = pl.BlockSpec((None, HEADS, tb, HEAD_DIM), cur_heads)
    prow = pl.BlockSpec((None, tb, D_MODEL), prev_rows)
    if first:
        x_specs = [pl.BlockSpec((None, tb, D_MODEL), prev_rows_before), prow,
                   _const_spec(xs[2].shape)]
    else:
        x_specs = [prow]
    nprob = nchunk * HEADS
    return pl.pallas_call(
        functools.partial(_delta_body, first=first, final=final, nchunk=nchunk, nblk=nblk,
                          row_block=tb),
        grid=(total + 1,),
        in_specs=x_specs + [heads, heads, heads, pl.BlockSpec((None, tb, LANES), cur_rows), prow, prow,
                  _const_spec(og.shape), _const_spec(wout.shape), _const_spec(fg.shape)],
        out_specs=prow,
        out_shape=jax.ShapeDtypeStruct((bsz, lp, D_MODEL), F32),
        scratch_shapes=[pltpu.VMEM((HEADS, HEAD_DIM, HEAD_DIM), F32),
                        pltpu.VMEM((2, nprob, HEAD_DIM, HEAD_DIM), F32),
                        pltpu.VMEM((2, nprob, HEAD_DIM, HEAD_DIM), BF16),
                        pltpu.VMEM((2, nprob, CHUNK, HEAD_DIM), BF16),
                        pltpu.VMEM((2, nprob, CHUNK, HEAD_DIM), F32),
                        pltpu.VMEM((2, nchunk, HEADS, LANES), F32)],
        compiler_params=pltpu.CompilerParams(
            dimension_semantics=("arbitrary",), vmem_limit_bytes=VMEM_LIMIT),
        name="delta_final" if final else ("delta_first" if first else "delta"),
    )(*xs, q, k, v, gb, sz, osc, og, wout, fg)


def _pad_lanes(v, fill=0.0):
    return jnp.pad(v.astype(F32), (0, LANES - v.shape[0]), constant_values=fill)[None, :]


def kernel(x, meta_tokens, norm_g, w_in, dn_conv_w, dn_A_log, dn_dt_bias, dn_out_g, sc_conv_w,
           w_out, final_g):
    bsz, seq, d = x.shape
    depth = w_in.shape[0]
    dn = HEADS * HEAD_DIM
    sc = (w_in.shape[2] - 4 * dn - 2 * HEADS) // 4
    assert d == D_MODEL and sc == D_MODEL and dn == D_MODEL
    lp = PAD_FRONT + N_META + seq
    assert lp % ROW_BLOCK == 0 and lp % INPROJ_ROWS == 0

    xs = (x, x, meta_tokens.astype(x.dtype))

    o0 = 4 * dn + 2 * HEADS
    for l in range(depth):
        w = w_in[l]
        wqkv = w[:, :3 * dn].astype(BF16)
        wz = w[:, 3 * dn:4 * dn].astype(BF16)
        wab = jnp.pad(w[:, 4 * dn:o0], ((0, 0), (0, LANES - 2 * HEADS))).astype(BF16)
        wb = w[:, o0:o0 + sc].astype(BF16)
        wc = w[:, o0 + sc:o0 + 2 * sc].astype(BF16)
        wh = w[:, o0 + 2 * sc:o0 + 3 * sc].astype(BF16)
        wzb = w[:, o0 + 3 * sc:o0 + 4 * sc].astype(BF16)
        q, k, v, sz, osc, gb = _inproj(
            xs, lp, norm_g[l][None, :], wqkv, wz, wab, wb, wc, wh, wzb,
            dn_conv_w[l], sc_conv_w[l], _pad_lanes(dn_A_log[l]), _pad_lanes(dn_dt_bias[l]))
        xp = _delta(xs, q, k, v, gb, sz, osc, dn_out_g[l][None, :], w_out[l].astype(BF16),
                    final_g[None, :], final=(l == depth - 1))
        xs = (xp,)
    return xp[:, PAD_FRONT + N_META:]
```

```python
import functools

import jax
import jax.numpy as jnp
from jax import lax
from jax.experimental import pallas as pl
from jax.experimental.pallas import tpu as pltpu

F32 = jnp.float32
BF16 = jnp.bfloat16

D_MODEL = 1024
N_META = 16
HEADS = 8
HEAD_DIM = 128
DN_CONV = 4
SC_CONV = 3
CHUNK = 64
EPS = 1e-6
PAD_FRONT = (-N_META) % CHUNK
LANES = 128
SUBLANES = 8
ROW_BLOCK = 320
INPROJ_ROWS = 416
VMEM_LIMIT = 56 * 1024 * 1024


def _silu(x):
    h = 0.5 * x
    return h + h * jnp.tanh(h)


def _softplus(x):
    return jnp.maximum(x, 0.0) + jnp.log(1.0 + jnp.exp(-jnp.abs(x)))


def _const_spec(shape):
    nd = len(shape)
    return pl.BlockSpec(shape, lambda *_: (0,) * nd, pipeline_mode=pl.Buffered(1))


def _layer_spec(layer, rows, cols, col_block=0):
    return pl.BlockSpec((None, rows, cols), lambda *_: (layer, 0, col_block),
                        pipeline_mode=pl.Buffered(1))


def _row_spec(rows, cols):
    return pl.BlockSpec((None, rows, cols), lambda b, j: (b, j, 0))


def _first_layer_rows(xprev_ref, xcur_ref, meta_ref, j, rows, cols=slice(None)):
    front = PAD_FRONT + N_META
    header = jnp.concatenate([jnp.zeros((PAD_FRONT, D_MODEL), F32), meta_ref[...]], axis=0)[:, cols]
    head = jnp.where(j == 0, header, xprev_ref[rows - front:rows, cols])
    return jnp.concatenate([head, xcur_ref[0:rows - front, cols]], axis=0)


def _inproj_body(*refs, tb, first):
    if first:
        xprev_ref, xcur_ref, meta_ref = refs[:3]
        refs = refs[3:]
    else:
        x_ref = refs[0]
        refs = refs[1:]
    (ng_ref, wqkv_ref, wz_ref, wab_ref, wb_ref, wc_ref, wh_ref, wzb_ref,
     cw_ref, scw_ref, alog_ref, dtb_ref,
     q_ref, k_ref, v_ref, sz_ref, osc_ref, gb_ref,
     qkv_scr, sc_scr, conv_scr) = refs
    j = pl.program_id(1)
    tbp = tb // SUBLANES
    wide = 2 * LANES

    @pl.when(j == 0)
    def _():
        qkv_scr[:, 0:SUBLANES, :] = jnp.zeros((3 * HEADS, SUBLANES, LANES), F32)
        sc_scr[:, 0:SUBLANES, :] = jnp.zeros((HEADS, SUBLANES, LANES), F32)

    x = _first_layer_rows(xprev_ref, xcur_ref, meta_ref, j, tb) if first else x_ref[...]
    ms = jnp.mean(x * x, axis=-1, keepdims=True)
    hb = (x * lax.rsqrt(ms + EPS) * ng_ref[...]).astype(BF16)

    def proj(w_ref, n):
        return jnp.dot(hb, w_ref[:, n * wide:(n + 1) * wide], preferred_element_type=F32)

    def strided(ref, slab, start):
        return ref[slab, pl.ds(start, SUBLANES, stride=tbp), :]

    for n in range(3 * D_MODEL // wide):
        r = proj(wqkv_ref, n)
        qkv_scr[2 * n, SUBLANES:SUBLANES + tb, :] = r[:, :LANES]
        qkv_scr[2 * n + 1, SUBLANES:SUBLANES + tb, :] = r[:, LANES:]
    first = SUBLANES - (DN_CONV - 1)
    outs = (q_ref, k_ref, v_ref)
    for s in range(3 * HEADS):
        which, hh = divmod(s, HEADS)
        sl = slice(s * LANES, (s + 1) * LANES)
        w = [jnp.broadcast_to(cw_ref[t:t + 1, sl], (SUBLANES, LANES)) for t in range(DN_CONV)]
        for v in range(tbp):
            y = w[0] * strided(qkv_scr, s, first + v)
            for t in range(1, DN_CONV):
                y = y + w[t] * strided(qkv_scr, s, first + v + t)
            y = _silu(y)
            if which < 2:
                inv = lax.rsqrt(jnp.sum(y * y, axis=-1, keepdims=True) + EPS)
                y = y * (inv * (HEAD_DIM ** -0.5) if which == 0 else inv)
            outs[which][hh, pl.ds(v, SUBLANES, stride=tbp), :] = y
    qkv_scr[:, first:SUBLANES, :] = qkv_scr[:, first + tb:SUBLANES + tb, :]

    for n in range(D_MODEL // wide):
        sz_ref[:, n * wide:(n + 1) * wide] = _silu(proj(wz_ref, n))

    pab = jnp.dot(hb, wab_ref[...], preferred_element_type=F32)
    g = -jnp.exp(alog_ref[...]) * _softplus(pab + dtb_ref[...])
    beta = 1.0 / (1.0 + jnp.exp(-pab))
    lane = lax.broadcasted_iota(jnp.int32, (tb, LANES), 1)
    row = lax.broadcasted_iota(jnp.int32, (tb, LANES), 0) + j * tb
    gb = jnp.where(lane < HEADS, g, beta)
    gb_ref[...] = jnp.where(row >= PAD_FRONT, gb, 0.0)

    for n in range(D_MODEL // wide):
        u = proj(wc_ref, n) * proj(wh_ref, n)
        sc_scr[2 * n, SUBLANES:SUBLANES + tb, :] = u[:, :LANES]
        sc_scr[2 * n + 1, SUBLANES:SUBLANES + tb, :] = u[:, LANES:]
    first = SUBLANES - (SC_CONV - 1)
    for s in range(D_MODEL // LANES):
        sl = slice(s * LANES, (s + 1) * LANES)
        w = [jnp.broadcast_to(scw_ref[t:t + 1, sl], (SUBLANES, LANES)) for t in range(SC_CONV)]
        for v in range(tbp):
            y = w[0] * strided(sc_scr, s, first + v)
            for t in range(1, SC_CONV):
                y = y + w[t] * strided(sc_scr, s, first + v + t)
            conv_scr[s, pl.ds(v, SUBLANES, stride=tbp), :] = y
    sc_scr[:, first:SUBLANES, :] = sc_scr[:, first + tb:SUBLANES + tb, :]
    for n in range(D_MODEL // wide):
        conv = jnp.concatenate([conv_scr[2 * n], conv_scr[2 * n + 1]], axis=1)
        y = proj(wb_ref, n) * conv * _silu(proj(wzb_ref, n))
        osc_ref[:, n * wide:(n + 1) * wide] = y.astype(BF16)


def _inproj(xs, lp, layer, ng, w_all, w_sc, cw, scw, alog, dtb):
    first = len(xs) == 3
    bsz = xs[0].shape[0]
    tb = INPROJ_ROWS
    row = _row_spec(tb, D_MODEL)
    qkv_cols = 3 * D_MODEL
    w_specs = [_layer_spec(layer, D_MODEL, qkv_cols),
               _layer_spec(layer, D_MODEL, D_MODEL, qkv_cols // D_MODEL),
               _layer_spec(layer, D_MODEL, LANES, (qkv_cols + D_MODEL) // LANES)]
    w_specs += [_layer_spec(layer, D_MODEL, D_MODEL, n) for n in range(4)]
    weights = (w_all, w_all, w_all, w_sc, w_sc, w_sc, w_sc)
    if first:
        x_specs = [pl.BlockSpec((None, tb, D_MODEL), lambda b, j: (b, jnp.maximum(j - 1, 0), 0)),
                   row, _const_spec(xs[2].shape)]
    else:
        x_specs = [row]
    heads = pl.BlockSpec((None, HEADS, tb, HEAD_DIM), lambda b, j: (b, 0, j, 0))
    per_head = jax.ShapeDtypeStruct((bsz, HEADS, lp, HEAD_DIM), F32)
    return pl.pallas_call(
        functools.partial(_inproj_body, tb=tb, first=first),
        grid=(bsz, lp // tb),
        in_specs=(x_specs + [_const_spec(ng.shape)] + w_specs
                  + [_const_spec(a.shape) for a in (cw, scw, alog, dtb)]),
        out_specs=[heads, heads, heads, row, row, _row_spec(tb, LANES)],
        out_shape=[per_head, per_head, per_head,
                   jax.ShapeDtypeStruct((bsz, lp, D_MODEL), F32),
                   jax.ShapeDtypeStruct((bsz, lp, D_MODEL), BF16),
                   jax.ShapeDtypeStruct((bsz, lp, LANES), F32)],
        scratch_shapes=[pltpu.VMEM((3 * HEADS, SUBLANES + tb, LANES), F32),
                        pltpu.VMEM((HEADS, SUBLANES + tb, LANES), F32),
                        pltpu.VMEM((HEADS, tb, LANES), F32)],
        compiler_params=pltpu.CompilerParams(
            dimension_semantics=("arbitrary", "arbitrary"), vmem_limit_bytes=VMEM_LIMIT),
        name="inproj_first" if first else "inproj",
    )(*xs, ng, *weights, cw, scw, alog, dtb)


def _pair_masks():
    ri = lax.broadcasted_iota(jnp.int32, (CHUNK, LANES), 0)
    li = lax.broadcasted_iota(jnp.int32, (CHUNK, LANES), 1)
    left = li < CHUNK
    lj = jnp.where(left, li, li - CHUNK)
    return left, ri >= lj, ri > lj


def _pair_cols(left, x, a, b):
    return jnp.where(left, x[:, a:a + 1], x[:, b:b + 1])


def _decay_terms(gb_ref, nchunk):
    left, causal, _ = _pair_masks()
    r64 = lax.broadcasted_iota(jnp.int32, (CHUNK, CHUNK), 0)
    c64 = lax.broadcasted_iota(jnp.int32, (CHUNK, CHUNK), 1)
    ltri = (r64 >= c64).astype(BF16)
    e_g, e_kd, gam, dec = [], [], [], []
    for c in range(nchunk):
        g = gb_ref[c * CHUNK:(c + 1) * CHUNK, :]
        hi = g.astype(BF16)
        rest = g - hi.astype(F32)
        mid = rest.astype(BF16)
        lo = (rest - mid.astype(F32)).astype(BF16)
        gcum = (jnp.dot(ltri, hi, preferred_element_type=F32)
                + jnp.dot(ltri, mid, preferred_element_type=F32)
                + jnp.dot(ltri, lo, preferred_element_type=F32))
        grow = jnp.concatenate([gcum, gcum], axis=0).T[0:HEADS, :]
        e_g.append(jnp.exp(gcum))
        e_kd.append(jnp.exp(gcum[CHUNK - 1:CHUNK, :] - gcum))
        gam.append(jnp.broadcast_to(jnp.exp(grow[:, CHUNK - 1:CHUNK]), (HEADS, LANES)))
        for p in range(HEADS // 2):
            diff = (_pair_cols(left, gcum, 2 * p, 2 * p + 1)
                    - jnp.where(left, grow[2 * p:2 * p + 1, :], grow[2 * p + 1:2 * p + 2, :]))
            dec.append(jnp.where(causal, jnp.exp(diff), 0.0))
    return e_g, e_kd, gam, dec


def _chunk_terms(q_ref, k_ref, v_ref, gb_ref, decay, nchunk, hooks=()):
    hooks = list(hooks)

    def stage_done():
        if hooks:
            for fn in hooks.pop(0):
                fn()

    e_g, e_kd, _, dec = decay
    left, _, strict = _pair_masks()
    zero_k = jnp.zeros((CHUNK, HEAD_DIM), BF16)

    def block_diag(p):
        return jnp.concatenate([jnp.where(left, p, 0.0), jnp.where(left, 0.0, p)],
                               axis=0).astype(BF16)

    rows = [slice(c * CHUNK, (c + 1) * CHUNK) for c in range(nchunk)]
    gbc = [gb_ref[rows[c], :] for c in range(nchunk)]

    probs = [(c, h) for c in range(nchunk) for h in range(HEADS)]
    pairs = [(c, p) for c in range(nchunk) for p in range(HEADS // 2)]
    ps = range(len(probs))
    qs = range(len(pairs))
    qh = [q_ref[h, rows[c], :] for c, h in probs]
    kh = [k_ref[h, rows[c], :] for c, h in probs]
    bcol = [gbc[c][:, HEADS + h:HEADS + h + 1] for c, h in probs]
    egc = [e_g[c][:, h:h + 1] for c, h in probs]
    beta2 = [_pair_cols(left, gbc[c], HEADS + 2 * p, HEADS + 2 * p + 1) for c, p in pairs]
    qkk = []
    for i in qs:
        a, b = 2 * i, 2 * i + 1
        lhs = jnp.concatenate([jnp.concatenate([qh[a], qh[b]], axis=1),
                               jnp.concatenate([kh[a], kh[b]], axis=1)], axis=0).astype(BF16)
        ka, kb = kh[a].astype(BF16), kh[b].astype(BF16)
        rhs_t = jnp.concatenate([jnp.concatenate([ka, zero_k], axis=1),
                                 jnp.concatenate([zero_k, kb], axis=1)], axis=0)
        qkk.append(lax.dot_general(lhs, rhs_t, (((1,), (1,)), ((), ())),
                                   preferred_element_type=F32))
    stage_done()
    xm = [jnp.where(strict, -(beta2[i] * qkk[i][CHUNK:] * dec[i]), 0.0) for i in qs]
    pw = [jnp.dot(xm[i].astype(BF16), block_diag(xm[i]), preferred_element_type=F32) for i in qs]
    stage_done()
    for _ in range(4):
        both = [jnp.dot(jnp.concatenate([xm[i], pw[i]], axis=0).astype(BF16), block_diag(pw[i]),
                        preferred_element_type=F32) for i in qs]
        stage_done()
        xm = [xm[i] + pw[i] + both[i][:CHUNK] for i in qs]
        pw = [both[i][CHUNK:] for i in qs]
    last = [jnp.dot(xm[i].astype(BF16), block_diag(pw[i]), preferred_element_type=F32) for i in qs]
    stage_done()
    xmb = [(xm[i] + pw[i] + last[i]).astype(BF16) for i in qs]
    pmb = [(qkk[i][:CHUNK] * dec[i]).astype(BF16) for i in qs]

    def pad_rows(m, p):
        z = jnp.zeros_like(m)
        return jnp.concatenate([m, z] if p % 2 == 0 else [z, m], axis=0)

    rhs = [jnp.concatenate([bcol[p] * v_ref[h, rows[c], :], (bcol[p] * egc[p]) * kh[p]], axis=1)
           for p, (c, h) in enumerate(probs)]
    xr = [jnp.dot(xmb[p // 2], pad_rows(rhs[p].astype(BF16), p), preferred_element_type=F32)
          for p in ps]
    stage_done()
    uw = [(rhs[p] + xr[p]).astype(BF16) for p in ps]
    cg = [lax.dot_general((kh[p] * e_kd[c][:, h:h + 1]).astype(BF16), uw[p],
                          (((0,), (0,)), ((), ())), preferred_element_type=F32)
          for p, (c, h) in enumerate(probs)]
    stage_done()
    op = [jnp.dot(pmb[p // 2], pad_rows(uw[p], p), preferred_element_type=F32) for p in ps]
    stage_done()
    while hooks:
        stage_done()
    cm = [cg[p][:, :HEAD_DIM] for p in ps]
    gm = [cg[p][:, HEAD_DIM:].astype(BF16) for p in ps]
    oin = [op[p][:, :HEAD_DIM] for p in ps]
    qt = [(qh[p] * egc[p] - op[p][:, HEAD_DIM:]).astype(BF16) for p in ps]
    return cm, gm, qt, oin


def _delta_body(*refs, first, final, nchunk, nblk, row_block):
    if first:
        xprev_ref, xcur_ref, meta_ref = refs[:3]
        refs = refs[3:]
    else:
        x_ref = refs[0]
        refs = refs[1:]
    (q_ref, k_ref, v_ref, gb_ref, sz_ref, osc_ref, og_ref, wout_ref, fg_ref,
     y_ref, s_scr, cm_scr, gm_scr, qt_scr, oin_scr, gam_scr) = refs
    t = pl.program_id(0)
    cur = lax.rem(t, 2)
    prev = 1 - cur
    jprev = lax.rem(jnp.maximum(t - 1, 0), nblk)

    @pl.when(t == 0)
    def _():
        s_scr[...] = jnp.zeros(s_scr.shape, F32)
        cm_scr[1] = jnp.zeros(cm_scr.shape[1:], F32)
        gm_scr[1] = jnp.zeros(gm_scr.shape[1:], BF16)
        qt_scr[1] = jnp.zeros(qt_scr.shape[1:], BF16)
        oin_scr[1] = jnp.zeros(oin_scr.shape[1:], F32)
        gam_scr[1] = jnp.zeros(gam_scr.shape[1:], F32)

    hs = range(HEADS)
    sls = [slice(h * HEAD_DIM, (h + 1) * HEAD_DIM) for h in hs]
    first_block = jprev == 0
    s = [jnp.where(first_block, 0.0, s_scr[h]) for h in hs]
    o = [[] for _ in hs]
    mix = []
    pieces = []
    wide = 2 * LANES

    def scan_step(c):
        def run():
            ps = [c * HEADS + h for h in hs]
            gam = gam_scr[prev, c]
            r = [jnp.dot(jnp.concatenate([gm_scr[prev, ps[h]], qt_scr[prev, ps[h]]], axis=0),
                         s[h].astype(BF16), preferred_element_type=F32) for h in hs]
            for h in hs:
                o[h].append(oin_scr[prev, ps[h]] + r[h][HEAD_DIM:])
                s[h] = gam[h:h + 1, :] * s[h] + cm_scr[prev, ps[h]] - r[h][:HEAD_DIM]
        return run

    def gate():
        for h in hs:
            s_scr[h] = s[h]
        parts = []
        for h in hs:
            oh = jnp.concatenate(o[h], axis=0)
            on = oh * lax.rsqrt(jnp.mean(oh * oh, axis=-1, keepdims=True) + EPS) * og_ref[...]
            parts.append((on * sz_ref[:, sls[h]]).astype(BF16))
        parts.append(osc_ref[...])
        mix.append(jnp.concatenate(parts, axis=1))

    def out_piece(n):
        def run():
            cols = slice(n * wide, (n + 1) * wide)
            if first:
                res = _first_layer_rows(xprev_ref, xcur_ref, meta_ref, jprev, row_block, cols)
            else:
                res = x_ref[:, cols]
            y = res + jnp.dot(mix[0], wout_ref[:, cols], preferred_element_type=F32)
            row = lax.broadcasted_iota(jnp.int32, y.shape, 0) + jprev * row_block
            y = jnp.where(row >= PAD_FRONT, y, 0.0)
            if final:
                pieces.append(y)
            else:
                y_ref[:, cols] = y
        return run

    steps = [scan_step(c) for c in range(nchunk)]
    outs = [out_piece(n) for n in range(D_MODEL // wide)]
    hooks = [steps[:4], steps[4:], [gate]] + [[fn] for fn in outs]
    decay = _decay_terms(gb_ref, nchunk)
    cm, gm, qt, oin = _chunk_terms(q_ref, k_ref, v_ref, gb_ref, decay, nchunk, hooks)
    if final:
        y = jnp.concatenate(pieces, axis=1)
        y_ref[...] = y * lax.rsqrt(jnp.mean(y * y, axis=-1, keepdims=True) + EPS) * fg_ref[...]

    for p in range(nchunk * HEADS):
        cm_scr[cur, p] = cm[p]
        gm_scr[cur, p] = gm[p]
        qt_scr[cur, p] = qt[p]
        oin_scr[cur, p] = oin[p]
    for c in range(nchunk):
        gam_scr[cur, c] = decay[2][c]


def _delta(xs, q, k, v, gb, sz, osc, layer, og, wout, fg, final):
    first = len(xs) == 3
    bsz, _, lp, _ = q.shape
    tb = ROW_BLOCK
    nchunk = tb // CHUNK
    nblk = lp // tb
    total = bsz * nblk

    def cur_idx(t):
        tc = jnp.minimum(t, total - 1)
        return tc // nblk, tc % nblk

    def prev_idx(t):
        tp = jnp.maximum(t - 1, 0)
        return tp // nblk, tp % nblk

    def cur_heads(t):
        b, j = cur_idx(t)
        return b, 0, j, 0

    def cur_rows(t):
        b, j = cur_idx(t)
        return b, j, 0

    def prev_rows(t):
        b, j = prev_idx(t)
        return b, j, 0

    def prev_rows_before(t):
        b, j = prev_idx(t)
        return b, jnp.maximum(j - 1, 0), 0

    heads = pl.BlockSpec((None, HEADS, tb, HEAD_DIM), cur_heads)
    prow = pl.BlockSpec((None, tb, D_MODEL), prev_rows)
    if first:
        x_specs = [pl.BlockSpec((None, tb, D_MODEL), prev_rows_before), prow,
                   _const_spec(xs[2].shape)]
    else:
        x_specs = [prow]
    nprob = nchunk * HEADS
    return pl.pallas_call(
        functools.partial(_delta_body, first=first, final=final, nchunk=nchunk, nblk=nblk,
                          row_block=tb),
        grid=(total + 1,),
        in_specs=x_specs + [heads, heads, heads, pl.BlockSpec((None, tb, LANES), cur_rows), prow, prow,
                  _const_spec(og.shape), _layer_spec(layer, 2 * D_MODEL, D_MODEL),
                  _const_spec(fg.shape)],
        out_specs=prow,
        out_shape=jax.ShapeDtypeStruct((bsz, lp, D_MODEL), F32),
        scratch_shapes=[pltpu.VMEM((HEADS, HEAD_DIM, HEAD_DIM), F32),
                        pltpu.VMEM((2, nprob, HEAD_DIM, HEAD_DIM), F32),
                        pltpu.VMEM((2, nprob, HEAD_DIM, HEAD_DIM), BF16),
                        pltpu.VMEM((2, nprob, CHUNK, HEAD_DIM), BF16),
                        pltpu.VMEM((2, nprob, CHUNK, HEAD_DIM), F32),
                        pltpu.VMEM((2, nchunk, HEADS, LANES), F32)],
        compiler_params=pltpu.CompilerParams(
            dimension_semantics=("arbitrary",), vmem_limit_bytes=VMEM_LIMIT),
        name="delta_final" if final else ("delta_first" if first else "delta"),
    )(*xs, q, k, v, gb, sz, osc, og, wout, fg)


def _pad_lanes(v, fill=0.0):
    return jnp.pad(v.astype(F32), (0, LANES - v.shape[0]), constant_values=fill)[None, :]


def kernel(x, meta_tokens, norm_g, w_in, dn_conv_w, dn_A_log, dn_dt_bias, dn_out_g, sc_conv_w,
           w_out, final_g):
    bsz, seq, d = x.shape
    depth = w_in.shape[0]
    dn = HEADS * HEAD_DIM
    sc = (w_in.shape[2] - 4 * dn - 2 * HEADS) // 4
    assert d == D_MODEL and sc == D_MODEL and dn == D_MODEL
    lp = PAD_FRONT + N_META + seq
    assert lp % ROW_BLOCK == 0 and lp % INPROJ_ROWS == 0

    xs = (x, x, meta_tokens.astype(x.dtype))

    w_all = w_in.astype(BF16)
    w_sc = w_all[:, :, 4 * dn + 2 * HEADS:]
    wout = w_out.astype(BF16)
    for l in range(depth):
        q, k, v, sz, osc, gb = _inproj(
            xs, lp, l, norm_g[l][None, :], w_all, w_sc,
            dn_conv_w[l], sc_conv_w[l], _pad_lanes(dn_A_log[l]), _pad_lanes(dn_dt_bias[l]))
        xp = _delta(xs, q, k, v, gb, sz, osc, l, dn_out_g[l][None, :], wout,
                    final_g[None, :], final=(l == depth - 1))
        xs = (xp,)
    return xp[:, PAD_FRONT + N_META:]
```

```python
import functools

import jax
import jax.numpy as jnp
from jax import lax
from jax.experimental import pallas as pl
from jax.experimental.pallas import tpu as pltpu

F32 = jnp.float32
BF16 = jnp.bfloat16

D_MODEL = 1024
N_META = 16
HEADS = 8
HEAD_DIM = 128
DN_CONV = 4
SC_CONV = 3
CHUNK = 64
EPS = 1e-6
PAD_FRONT = (-N_META) % CHUNK
LANES = 128
SUBLANES = 8
ROW_BLOCK = 320
INPROJ_ROWS = 416
VMEM_LIMIT = 56 * 1024 * 1024


def _silu(x):
    h = 0.5 * x
    return h + h * jnp.tanh(h)


def _softplus(x):
    return jnp.maximum(x, 0.0) + jnp.log(1.0 + jnp.exp(-jnp.abs(x)))


def _const_spec(shape):
    nd = len(shape)
    return pl.BlockSpec(shape, lambda *_: (0,) * nd, pipeline_mode=pl.Buffered(1))


def _layer_spec(layer, rows, cols, col_block=0):
    return pl.BlockSpec((None, rows, cols), lambda *_: (layer, 0, col_block),
                        pipeline_mode=pl.Buffered(1))


def _row_spec(rows, cols):
    return pl.BlockSpec((None, rows, cols), lambda b, j: (b, j, 0))


def _first_layer_rows(xprev_ref, xcur_ref, meta_ref, j, rows, cols=slice(None)):
    front = PAD_FRONT + N_META
    header = jnp.concatenate([jnp.zeros((PAD_FRONT, D_MODEL), F32), meta_ref[...]], axis=0)[:, cols]
    head = jnp.where(j == 0, header, xprev_ref[rows - front:rows, cols])
    return jnp.concatenate([head, xcur_ref[0:rows - front, cols]], axis=0)


def _inproj_body(*refs, tb, first):
    if first:
        xprev_ref, xcur_ref, meta_ref = refs[:3]
        refs = refs[3:]
    else:
        x_ref = refs[0]
        refs = refs[1:]
    (ng_ref, wqkv_ref, wz_ref, wab_ref, wb_ref, wc_ref, wh_ref, wzb_ref,
     cw_ref, scw_ref, alog_ref, dtb_ref,
     q_ref, k_ref, v_ref, sz_ref, osc_ref, gb_ref,
     qkv_scr, sc_scr, conv_scr) = refs
    j = pl.program_id(1)
    tbp = tb // SUBLANES
    wide = 2 * LANES

    @pl.when(j == 0)
    def _():
        qkv_scr[:, 0:SUBLANES, :] = jnp.zeros((3 * HEADS, SUBLANES, LANES), F32)
        sc_scr[:, 0:SUBLANES, :] = jnp.zeros((HEADS, SUBLANES, LANES), F32)

    x = _first_layer_rows(xprev_ref, xcur_ref, meta_ref, j, tb) if first else x_ref[...]
    ms = jnp.mean(x * x, axis=-1, keepdims=True)
    hb = (x * lax.rsqrt(ms + EPS) * ng_ref[...]).astype(BF16)

    def proj(w_ref, n):
        return jnp.dot(hb, w_ref[:, n * wide:(n + 1) * wide], preferred_element_type=F32)

    def strided(ref, slab, start):
        return ref[slab, pl.ds(start, SUBLANES, stride=tbp), :]

    for n in range(3 * D_MODEL // wide):
        r = proj(wqkv_ref, n)
        qkv_scr[2 * n, SUBLANES:SUBLANES + tb, :] = r[:, :LANES]
        qkv_scr[2 * n + 1, SUBLANES:SUBLANES + tb, :] = r[:, LANES:]
    first = SUBLANES - (DN_CONV - 1)
    outs = (q_ref, k_ref, v_ref)
    for s in range(3 * HEADS):
        which, hh = divmod(s, HEADS)
        sl = slice(s * LANES, (s + 1) * LANES)
        w = [jnp.broadcast_to(cw_ref[t:t + 1, sl], (SUBLANES, LANES)) for t in range(DN_CONV)]
        for v in range(tbp):
            y = w[0] * strided(qkv_scr, s, first + v)
            for t in range(1, DN_CONV):
                y = y + w[t] * strided(qkv_scr, s, first + v + t)
            y = _silu(y)
            if which < 2:
                inv = lax.rsqrt(jnp.sum(y * y, axis=-1, keepdims=True) + EPS)
                y = y * (inv * (HEAD_DIM ** -0.5) if which == 0 else inv)
            outs[which][hh, pl.ds(v, SUBLANES, stride=tbp), :] = y
    qkv_scr[:, first:SUBLANES, :] = qkv_scr[:, first + tb:SUBLANES + tb, :]

    for n in range(D_MODEL // wide):
        sz_ref[:, n * wide:(n + 1) * wide] = _silu(proj(wz_ref, n))

    pab = jnp.dot(hb, wab_ref[...], preferred_element_type=F32)
    g = -jnp.exp(alog_ref[...]) * _softplus(pab + dtb_ref[...])
    beta = 1.0 / (1.0 + jnp.exp(-pab))
    lane = lax.broadcasted_iota(jnp.int32, (tb, LANES), 1)
    row = lax.broadcasted_iota(jnp.int32, (tb, LANES), 0) + j * tb
    gb = jnp.where(lane < HEADS, g, beta)
    gb_ref[...] = jnp.where(row >= PAD_FRONT, gb, 0.0)

    for n in range(D_MODEL // wide):
        u = proj(wc_ref, n) * proj(wh_ref, n)
        sc_scr[2 * n, SUBLANES:SUBLANES + tb, :] = u[:, :LANES]
        sc_scr[2 * n + 1, SUBLANES:SUBLANES + tb, :] = u[:, LANES:]
    first = SUBLANES - (SC_CONV - 1)
    for s in range(D_MODEL // LANES):
        sl = slice(s * LANES, (s + 1) * LANES)
        w = [jnp.broadcast_to(scw_ref[t:t + 1, sl], (SUBLANES, LANES)) for t in range(SC_CONV)]
        for v in range(tbp):
            y = w[0] * strided(sc_scr, s, first + v)
            for t in range(1, SC_CONV):
                y = y + w[t] * strided(sc_scr, s, first + v + t)
            conv_scr[s, pl.ds(v, SUBLANES, stride=tbp), :] = y
    sc_scr[:, first:SUBLANES, :] = sc_scr[:, first + tb:SUBLANES + tb, :]
    for n in range(D_MODEL // wide):
        conv = jnp.concatenate([conv_scr[2 * n], conv_scr[2 * n + 1]], axis=1)
        y = proj(wb_ref, n) * conv * _silu(proj(wzb_ref, n))
        osc_ref[:, n * wide:(n + 1) * wide] = y.astype(BF16)


def _inproj(xs, lp, layer, ng, w_all, w_sc, cw, scw, alog, dtb):
    first = len(xs) == 3
    bsz = xs[0].shape[0]
    tb = INPROJ_ROWS
    row = _row_spec(tb, D_MODEL)
    qkv_cols = 3 * D_MODEL
    w_specs = [_layer_spec(layer, D_MODEL, qkv_cols),
               _layer_spec(layer, D_MODEL, D_MODEL, qkv_cols // D_MODEL),
               _layer_spec(layer, D_MODEL, LANES, (qkv_cols + D_MODEL) // LANES)]
    w_specs += [_layer_spec(layer, D_MODEL, D_MODEL, n) for n in range(4)]
    weights = (w_all, w_all, w_all, w_sc, w_sc, w_sc, w_sc)
    if first:
        x_specs = [pl.BlockSpec((None, tb, D_MODEL), lambda b, j: (b, jnp.maximum(j - 1, 0), 0)),
                   row, _const_spec(xs[2].shape)]
    else:
        x_specs = [row]
    heads = pl.BlockSpec((None, HEADS, tb, HEAD_DIM), lambda b, j: (b, 0, j, 0))
    per_head = jax.ShapeDtypeStruct((bsz, HEADS, lp, HEAD_DIM), F32)
    return pl.pallas_call(
        functools.partial(_inproj_body, tb=tb, first=first),
        grid=(bsz, lp // tb),
        in_specs=(x_specs + [_const_spec(ng.shape)] + w_specs
                  + [_const_spec(a.shape) for a in (cw, scw, alog, dtb)]),
        out_specs=[heads, heads, heads, row, row, _row_spec(tb, LANES)],
        out_shape=[per_head, per_head, per_head,
                   jax.ShapeDtypeStruct((bsz, lp, D_MODEL), F32),
                   jax.ShapeDtypeStruct((bsz, lp, D_MODEL), BF16),
                   jax.ShapeDtypeStruct((bsz, lp, LANES), F32)],
        scratch_shapes=[pltpu.VMEM((3 * HEADS, SUBLANES + tb, LANES), F32),
                        pltpu.VMEM((HEADS, SUBLANES + tb, LANES), F32),
                        pltpu.VMEM((HEADS, tb, LANES), F32)],
        compiler_params=pltpu.CompilerParams(
            dimension_semantics=("arbitrary", "arbitrary"), vmem_limit_bytes=VMEM_LIMIT),
        name="inproj_first" if first else "inproj",
    )(*xs, ng, *weights, cw, scw, alog, dtb)


def _pair_masks():
    ri = lax.broadcasted_iota(jnp.int32, (CHUNK, LANES), 0)
    li = lax.broadcasted_iota(jnp.int32, (CHUNK, LANES), 1)
    left = li < CHUNK
    lj = jnp.where(left, li, li - CHUNK)
    return left, ri >= lj, ri > lj


def _pair_cols(left, x, a, b):
    return jnp.where(left, x[:, a:a + 1], x[:, b:b + 1])


def _decay_terms(gb_ref, nchunk):
    left, causal, _ = _pair_masks()
    r64 = lax.broadcasted_iota(jnp.int32, (CHUNK, CHUNK), 0)
    c64 = lax.broadcasted_iota(jnp.int32, (CHUNK, CHUNK), 1)
    ltri = (r64 >= c64).astype(BF16)
    e_g, e_kd, gam, dec = [], [], [], []
    for c in range(nchunk):
        g = gb_ref[c * CHUNK:(c + 1) * CHUNK, :]
        hi = g.astype(BF16)
        rest = g - hi.astype(F32)
        mid = rest.astype(BF16)
        lo = (rest - mid.astype(F32)).astype(BF16)
        gcum = (jnp.dot(ltri, hi, preferred_element_type=F32)
                + jnp.dot(ltri, mid, preferred_element_type=F32)
                + jnp.dot(ltri, lo, preferred_element_type=F32))
        grow = jnp.concatenate([gcum, gcum], axis=0).T[0:HEADS, :]
        e_g.append(jnp.exp(gcum))
        e_kd.append(jnp.exp(gcum[CHUNK - 1:CHUNK, :] - gcum))
        gam.append(jnp.broadcast_to(jnp.exp(grow[:, CHUNK - 1:CHUNK]), (HEADS, LANES)))
        for p in range(HEADS // 2):
            diff = (_pair_cols(left, gcum, 2 * p, 2 * p + 1)
                    - jnp.where(left, grow[2 * p:2 * p + 1, :], grow[2 * p + 1:2 * p + 2, :]))
            dec.append(jnp.where(causal, jnp.exp(diff), 0.0))
    return e_g, e_kd, gam, dec


def _chunk_terms(q_ref, k_ref, v_ref, gb_ref, decay, nchunk, hooks=()):
    hooks = list(hooks)

    def stage_done():
        if hooks:
            for fn in hooks.pop(0):
                fn()

    e_g, e_kd, _, dec = decay
    left, _, strict = _pair_masks()
    zero_k = jnp.zeros((CHUNK, HEAD_DIM), BF16)

    def block_diag(p):
        return jnp.concatenate([jnp.where(left, p, 0.0), jnp.where(left, 0.0, p)],
                               axis=0).astype(BF16)

    rows = [slice(c * CHUNK, (c + 1) * CHUNK) for c in range(nchunk)]
    gbc = [gb_ref[rows[c], :] for c in range(nchunk)]

    probs = [(c, h) for c in range(nchunk) for h in range(HEADS)]
    pairs = [(c, p) for c in range(nchunk) for p in range(HEADS // 2)]
    ps = range(len(probs))
    qs = range(len(pairs))
    qh = [q_ref[h, rows[c], :] for c, h in probs]
    kh = [k_ref[h, rows[c], :] for c, h in probs]
    bcol = [gbc[c][:, HEADS + h:HEADS + h + 1] for c, h in probs]
    egc = [e_g[c][:, h:h + 1] for c, h in probs]
    beta2 = [_pair_cols(left, gbc[c], HEADS + 2 * p, HEADS + 2 * p + 1) for c, p in pairs]
    qkk = []
    for i in qs:
        a, b = 2 * i, 2 * i + 1
        lhs = jnp.concatenate([jnp.concatenate([qh[a], qh[b]], axis=1),
                               jnp.concatenate([kh[a], kh[b]], axis=1)], axis=0).astype(BF16)
        ka, kb = kh[a].astype(BF16), kh[b].astype(BF16)
        rhs_t = jnp.concatenate([jnp.concatenate([ka, zero_k], axis=1),
                                 jnp.concatenate([zero_k, kb], axis=1)], axis=0)
        qkk.append(lax.dot_general(lhs, rhs_t, (((1,), (1,)), ((), ())),
                                   preferred_element_type=F32))
    stage_done()
    xm = [jnp.where(strict, -(beta2[i] * qkk[i][CHUNK:] * dec[i]), 0.0) for i in qs]
    pw = [jnp.dot(xm[i].astype(BF16), block_diag(xm[i]), preferred_element_type=F32) for i in qs]
    stage_done()
    for _ in range(4):
        both = [jnp.dot(jnp.concatenate([xm[i], pw[i]], axis=0).astype(BF16), block_diag(pw[i]),
                        preferred_element_type=F32) for i in qs]
        stage_done()
        xm = [xm[i] + pw[i] + both[i][:CHUNK] for i in qs]
        pw = [both[i][CHUNK:] for i in qs]
    last = [jnp.dot(xm[i].astype(BF16), block_diag(pw[i]), preferred_element_type=F32) for i in qs]
    stage_done()
    xmb = [(xm[i] + pw[i] + last[i]).astype(BF16) for i in qs]
    pmb = [(qkk[i][:CHUNK] * dec[i]).astype(BF16) for i in qs]

    def pad_rows(m, p):
        z = jnp.zeros_like(m)
        return jnp.concatenate([m, z] if p % 2 == 0 else [z, m], axis=0)

    rhs = [jnp.concatenate([bcol[p] * v_ref[h, rows[c], :], (bcol[p] * egc[p]) * kh[p]], axis=1)
           for p, (c, h) in enumerate(probs)]
    xr = [jnp.dot(xmb[p // 2], pad_rows(rhs[p].astype(BF16), p), preferred_element_type=F32)
          for p in ps]
    stage_done()
    uw = [(rhs[p] + xr[p]).astype(BF16) for p in ps]
    cg = [lax.dot_general((kh[p] * e_kd[c][:, h:h + 1]).astype(BF16), uw[p],
                          (((0,), (0,)), ((), ())), preferred_element_type=F32)
          for p, (c, h) in enumerate(probs)]
    stage_done()
    op = [jnp.dot(pmb[p // 2], pad_rows(uw[p], p), preferred_element_type=F32) for p in ps]
    stage_done()
    while hooks:
        stage_done()
    cm = [cg[p][:, :HEAD_DIM] for p in ps]
    gm = [cg[p][:, HEAD_DIM:].astype(BF16) for p in ps]
    oin = [op[p][:, :HEAD_DIM] for p in ps]
    qt = [(qh[p] * egc[p] - op[p][:, HEAD_DIM:]).astype(BF16) for p in ps]
    return cm, gm, qt, oin


def _delta_body(*refs, first, final, nchunk, nblk, row_block):
    if first:
        xprev_ref, xcur_ref, meta_ref = refs[:3]
        refs = refs[3:]
    else:
        x_ref = refs[0]
        refs = refs[1:]
    (q_ref, k_ref, v_ref, gb_ref, sz_ref, osc_ref, og_ref, wout_ref, fg_ref,
     y_ref, s_scr, cm_scr, gm_scr, qt_scr, oin_scr, gam_scr) = refs[:16]
    carry_scr = refs[16] if final else None
    t = pl.program_id(0)
    cur = lax.rem(t, 2)
    prev = 1 - cur
    jprev = lax.rem(jnp.maximum(t - 1, 0), nblk)

    @pl.when(t == 0)
    def _():
        s_scr[...] = jnp.zeros(s_scr.shape, F32)
        cm_scr[1] = jnp.zeros(cm_scr.shape[1:], F32)
        gm_scr[1] = jnp.zeros(gm_scr.shape[1:], BF16)
        qt_scr[1] = jnp.zeros(qt_scr.shape[1:], BF16)
        oin_scr[1] = jnp.zeros(oin_scr.shape[1:], F32)
        gam_scr[1] = jnp.zeros(gam_scr.shape[1:], F32)
        if final:
            carry_scr[...] = jnp.zeros(carry_scr.shape, F32)

    hs = range(HEADS)
    sls = [slice(h * HEAD_DIM, (h + 1) * HEAD_DIM) for h in hs]
    first_block = jprev == 0
    s = [jnp.where(first_block, 0.0, s_scr[h]) for h in hs]
    o = [[] for _ in hs]
    mix = []
    pieces = []
    wide = 2 * LANES

    def scan_step(c):
        def run():
            ps = [c * HEADS + h for h in hs]
            gam = gam_scr[prev, c]
            r = [jnp.dot(jnp.concatenate([gm_scr[prev, ps[h]], qt_scr[prev, ps[h]]], axis=0),
                         s[h].astype(BF16), preferred_element_type=F32) for h in hs]
            for h in hs:
                o[h].append(oin_scr[prev, ps[h]] + r[h][HEAD_DIM:])
                s[h] = gam[h:h + 1, :] * s[h] + cm_scr[prev, ps[h]] - r[h][:HEAD_DIM]
        return run

    def gate():
        for h in hs:
            s_scr[h] = s[h]
        parts = []
        for h in hs:
            oh = jnp.concatenate(o[h], axis=0)
            on = oh * lax.rsqrt(jnp.mean(oh * oh, axis=-1, keepdims=True) + EPS) * og_ref[...]
            parts.append((on * sz_ref[:, sls[h]]).astype(BF16))
        parts.append(osc_ref[...])
        mix.append(jnp.concatenate(parts, axis=1))

    def out_piece(n):
        def run():
            cols = slice(n * wide, (n + 1) * wide)
            if first:
                res = _first_layer_rows(xprev_ref, xcur_ref, meta_ref, jprev, row_block, cols)
            else:
                res = x_ref[:, cols]
            y = res + jnp.dot(mix[0], wout_ref[:, cols], preferred_element_type=F32)
            row = lax.broadcasted_iota(jnp.int32, y.shape, 0) + jprev * row_block
            y = jnp.where(row >= PAD_FRONT, y, 0.0)
            if final:
                pieces.append(y)
            else:
                y_ref[:, cols] = y
        return run

    steps = [scan_step(c) for c in range(nchunk)]
    outs = [out_piece(n) for n in range(D_MODEL // wide)]
    hooks = [steps[:4], steps[4:], [gate]] + [[fn] for fn in outs]
    decay = _decay_terms(gb_ref, nchunk)
    cm, gm, qt, oin = _chunk_terms(q_ref, k_ref, v_ref, gb_ref, decay, nchunk, hooks)
    if final:
        front = PAD_FRONT + N_META
        y = jnp.concatenate(pieces, axis=1)
        y = y * lax.rsqrt(jnp.mean(y * y, axis=-1, keepdims=True) + EPS) * fg_ref[...]
        y_ref[0:row_block - front, :] = carry_scr[...]
        y_ref[row_block - front:row_block, :] = y[0:front]
        carry_scr[...] = y[front:row_block]

    for p in range(nchunk * HEADS):
        cm_scr[cur, p] = cm[p]
        gm_scr[cur, p] = gm[p]
        qt_scr[cur, p] = qt[p]
        oin_scr[cur, p] = oin[p]
    for c in range(nchunk):
        gam_scr[cur, c] = decay[2][c]


def _delta(xs, q, k, v, gb, sz, osc, layer, og, wout, fg, final):
    first = len(xs) == 3
    bsz, _, lp, _ = q.shape
    tb = ROW_BLOCK
    nchunk = tb // CHUNK
    nblk = lp // tb
    total = bsz * nblk
    front = PAD_FRONT + N_META

    def cur_idx(t):
        tc = jnp.minimum(t, total - 1)
        return tc // nblk, tc % nblk

    def prev_idx(t):
        tp = jnp.clip(t - 1, 0, total - 1)
        return tp // nblk, tp % nblk

    def cur_heads(t):
        b, j = cur_idx(t)
        return b, 0, j, 0

    def cur_rows(t):
        b, j = cur_idx(t)
        return b, j, 0

    def prev_rows(t):
        b, j = prev_idx(t)
        return b, j, 0

    def prev_rows_before(t):
        b, j = prev_idx(t)
        return b, jnp.maximum(j - 1, 0), 0

    def final_rows(t):
        tp = jnp.clip(t - 2, 0, total - 1)
        return tp // nblk, tp % nblk, 0

    heads = pl.BlockSpec((None, HEADS, tb, HEAD_DIM), cur_heads)
    prow = pl.BlockSpec((None, tb, D_MODEL), prev_rows)
    if first:
        x_specs = [pl.BlockSpec((None, tb, D_MODEL), prev_rows_before), prow,
                   _const_spec(xs[2].shape)]
    else:
        x_specs = [prow]
    nprob = nchunk * HEADS
    return pl.pallas_call(
        functools.partial(_delta_body, first=first, final=final, nchunk=nchunk, nblk=nblk,
                          row_block=tb),
        grid=(total + 2 if final else total + 1,),
        in_specs=x_specs + [heads, heads, heads, pl.BlockSpec((None, tb, LANES), cur_rows), prow, prow,
                  _const_spec(og.shape), _layer_spec(layer, 2 * D_MODEL, D_MODEL),
                  _const_spec(fg.shape)],
        out_specs=pl.BlockSpec((None, tb, D_MODEL), final_rows) if final else prow,
        out_shape=jax.ShapeDtypeStruct((bsz, lp - front if final else lp, D_MODEL), F32),
        scratch_shapes=[pltpu.VMEM((HEADS, HEAD_DIM, HEAD_DIM), F32),
                        pltpu.VMEM((2, nprob, HEAD_DIM, HEAD_DIM), F32),
                        pltpu.VMEM((2, nprob, HEAD_DIM, HEAD_DIM), BF16),
                        pltpu.VMEM((2, nprob, CHUNK, HEAD_DIM), BF16),
                        pltpu.VMEM((2, nprob, CHUNK, HEAD_DIM), F32),
                        pltpu.VMEM((2, nchunk, HEADS, LANES), F32)]
        + ([pltpu.VMEM((tb - front, D_MODEL), F32)] if final else []),
        compiler_params=pltpu.CompilerParams(
            dimension_semantics=("arbitrary",), vmem_limit_bytes=VMEM_LIMIT),
        name="delta_final" if final else ("delta_first" if first else "delta"),
    )(*xs, q, k, v, gb, sz, osc, og, wout, fg)


def _pad_lanes(v, fill=0.0):
    return jnp.pad(v.astype(F32), (0, LANES - v.shape[0]), constant_values=fill)[None, :]


def kernel(x, meta_tokens, norm_g, w_in, dn_conv_w, dn_A_log, dn_dt_bias, dn_out_g, sc_conv_w,
           w_out, final_g):
    bsz, seq, d = x.shape
    depth = w_in.shape[0]
    dn = HEADS * HEAD_DIM
    sc = (w_in.shape[2] - 4 * dn - 2 * HEADS) // 4
    assert d == D_MODEL and sc == D_MODEL and dn == D_MODEL
    lp = PAD_FRONT + N_META + seq
    assert lp % ROW_BLOCK == 0 and lp % INPROJ_ROWS == 0

    xs = (x, x, meta_tokens.astype(x.dtype))

    w_all = w_in.astype(BF16)
    w_sc = w_all[:, :, 4 * dn + 2 * HEADS:]
    wout = w_out.astype(BF16)
    for l in range(depth):
        q, k, v, sz, osc, gb = _inproj(
            xs, lp, l, norm_g[l][None, :], w_all, w_sc,
            dn_conv_w[l], sc_conv_w[l], _pad_lanes(dn_A_log[l]), _pad_lanes(dn_dt_bias[l]))
        xp = _delta(xs, q, k, v, gb, sz, osc, l, dn_out_g[l][None, :], wout,
                    final_g[None, :], final=(l == depth - 1))
        xs = (xp,)
    return xp
```

```python
import functools

import jax
import jax.numpy as jnp
from jax import lax
from jax.experimental import pallas as pl
from jax.experimental.pallas import tpu as pltpu

F32 = jnp.float32
BF16 = jnp.bfloat16

D_MODEL = 1024
N_META = 16
HEADS = 8
HEAD_DIM = 128
DN_CONV = 4
SC_CONV = 3
CHUNK = 64
EPS = 1e-6
PAD_FRONT = (-N_META) % CHUNK
LANES = 128
SUBLANES = 8
ROW_BLOCK = 320
INPROJ_ROWS = 416
VMEM_LIMIT = 56 * 1024 * 1024


def _silu(x):
    h = 0.5 * x
    return h + h * jnp.tanh(h)


def _softplus(x):
    return jnp.maximum(x, 0.0) + jnp.log(1.0 + jnp.exp(-jnp.abs(x)))


def _const_spec(shape):
    nd = len(shape)
    return pl.BlockSpec(shape, lambda *_: (0,) * nd, pipeline_mode=pl.Buffered(1))


def _layer_spec(layer, rows, cols, col_block=0):
    return pl.BlockSpec((None, rows, cols), lambda *_: (layer, 0, col_block),
                        pipeline_mode=pl.Buffered(1))


def _row_spec(rows, cols):
    return pl.BlockSpec((None, rows, cols), lambda b, j: (b, j, 0))


def _first_layer_rows(xprev_ref, xcur_ref, meta_ref, j, rows, cols=slice(None)):
    front = PAD_FRONT + N_META
    header = jnp.concatenate([jnp.zeros((PAD_FRONT, D_MODEL), F32), meta_ref[...]], axis=0)[:, cols]
    head = jnp.where(j == 0, header, xprev_ref[rows - front:rows, cols])
    return jnp.concatenate([head, xcur_ref[0:rows - front, cols]], axis=0)


def _inproj_body(*refs, tb, first):
    if first:
        xprev_ref, xcur_ref, meta_ref = refs[:3]
        refs = refs[3:]
    else:
        x_ref = refs[0]
        refs = refs[1:]
    (ng_ref, wqkv_ref, wz_ref, wab_ref, wb_ref, wc_ref, wh_ref, wzb_ref,
     cw_ref, scw_ref, alog_ref, dtb_ref,
     q_ref, k_ref, v_ref, sz_ref, osc_ref, gb_ref,
     qkv_scr, sc_scr, conv_scr) = refs
    j = pl.program_id(1)
    tbp = tb // SUBLANES
    wide = 2 * LANES

    @pl.when(j == 0)
    def _():
        qkv_scr[:, 0:SUBLANES, :] = jnp.zeros((3 * HEADS, SUBLANES, LANES), F32)
        sc_scr[:, 0:SUBLANES, :] = jnp.zeros((HEADS, SUBLANES, LANES), F32)

    x = _first_layer_rows(xprev_ref, xcur_ref, meta_ref, j, tb) if first else x_ref[...]
    ms = jnp.mean(x * x, axis=-1, keepdims=True)
    hb = (x * lax.rsqrt(ms + EPS) * ng_ref[...]).astype(BF16)

    def proj(w_ref, n):
        return jnp.dot(hb, w_ref[:, n * wide:(n + 1) * wide], preferred_element_type=F32)

    def strided(ref, slab, start):
        return ref[slab, pl.ds(start, SUBLANES, stride=tbp), :]

    for n in range(3 * D_MODEL // wide):
        r = proj(wqkv_ref, n)
        qkv_scr[2 * n, SUBLANES:SUBLANES + tb, :] = r[:, :LANES]
        qkv_scr[2 * n + 1, SUBLANES:SUBLANES + tb, :] = r[:, LANES:]
    first = SUBLANES - (DN_CONV - 1)
    outs = (q_ref, k_ref, v_ref)
    for s in range(3 * HEADS):
        which, hh = divmod(s, HEADS)
        sl = slice(s * LANES, (s + 1) * LANES)
        w = [jnp.broadcast_to(cw_ref[t:t + 1, sl], (SUBLANES, LANES)) for t in range(DN_CONV)]
        win = [strided(qkv_scr, s, first + u) for u in range(tbp + DN_CONV - 1)]
        for v in range(tbp):
            y = w[0] * win[v]
            for t in range(1, DN_CONV):
                y = y + w[t] * win[v + t]
            y = _silu(y)
            if which < 2:
                inv = lax.rsqrt(jnp.sum(y * y, axis=-1, keepdims=True) + EPS)
                y = y * (inv * (HEAD_DIM ** -0.5) if which == 0 else inv)
            outs[which][hh, pl.ds(v, SUBLANES, stride=tbp), :] = y
    qkv_scr[:, first:SUBLANES, :] = qkv_scr[:, first + tb:SUBLANES + tb, :]

    for n in range(D_MODEL // wide):
        sz_ref[:, n * wide:(n + 1) * wide] = _silu(proj(wz_ref, n))

    pab = jnp.dot(hb, wab_ref[...], preferred_element_type=F32)
    g = -jnp.exp(alog_ref[...]) * _softplus(pab + dtb_ref[...])
    beta = 1.0 / (1.0 + jnp.exp(-pab))
    lane = lax.broadcasted_iota(jnp.int32, (tb, LANES), 1)
    row = lax.broadcasted_iota(jnp.int32, (tb, LANES), 0) + j * tb
    gb = jnp.where(lane < HEADS, g, beta)
    gb_ref[...] = jnp.where(row >= PAD_FRONT, gb, 0.0)

    for n in range(D_MODEL // wide):
        u = proj(wc_ref, n) * proj(wh_ref, n)
        sc_scr[2 * n, SUBLANES:SUBLANES + tb, :] = u[:, :LANES]
        sc_scr[2 * n + 1, SUBLANES:SUBLANES + tb, :] = u[:, LANES:]
    first = SUBLANES - (SC_CONV - 1)
    for s in range(D_MODEL // LANES):
        sl = slice(s * LANES, (s + 1) * LANES)
        w = [jnp.broadcast_to(scw_ref[t:t + 1, sl], (SUBLANES, LANES)) for t in range(SC_CONV)]
        win = [strided(sc_scr, s, first + u) for u in range(tbp + SC_CONV - 1)]
        for v in range(tbp):
            y = w[0] * win[v]
            for t in range(1, SC_CONV):
                y = y + w[t] * win[v + t]
            conv_scr[s, pl.ds(v, SUBLANES, stride=tbp), :] = y
    sc_scr[:, first:SUBLANES, :] = sc_scr[:, first + tb:SUBLANES + tb, :]
    for n in range(D_MODEL // wide):
        conv = jnp.concatenate([conv_scr[2 * n], conv_scr[2 * n + 1]], axis=1)
        y = proj(wb_ref, n) * conv * _silu(proj(wzb_ref, n))
        osc_ref[:, n * wide:(n + 1) * wide] = y.astype(BF16)


def _inproj(xs, lp, layer, ng, w_all, w_sc, cw, scw, alog, dtb):
    first = len(xs) == 3
    bsz = xs[0].shape[0]
    tb = INPROJ_ROWS
    row = _row_spec(tb, D_MODEL)
    qkv_cols = 3 * D_MODEL
    w_specs = [_layer_spec(layer, D_MODEL, qkv_cols),
               _layer_spec(layer, D_MODEL, D_MODEL, qkv_cols // D_MODEL),
               _layer_spec(layer, D_MODEL, LANES, (qkv_cols + D_MODEL) // LANES)]
    w_specs += [_layer_spec(layer, D_MODEL, D_MODEL, n) for n in range(4)]
    weights = (w_all, w_all, w_all, w_sc, w_sc, w_sc, w_sc)
    if first:
        x_specs = [pl.BlockSpec((None, tb, D_MODEL), lambda b, j: (b, jnp.maximum(j - 1, 0), 0)),
                   row, _const_spec(xs[2].shape)]
    else:
        x_specs = [row]
    heads = pl.BlockSpec((None, HEADS, tb, HEAD_DIM), lambda b, j: (b, 0, j, 0))
    per_head = jax.ShapeDtypeStruct((bsz, HEADS, lp, HEAD_DIM), F32)
    return pl.pallas_call(
        functools.partial(_inproj_body, tb=tb, first=first),
        grid=(bsz, lp // tb),
        in_specs=(x_specs + [_const_spec(ng.shape)] + w_specs
                  + [_const_spec(a.shape) for a in (cw, scw, alog, dtb)]),
        out_specs=[heads, heads, heads, row, row, _row_spec(tb, LANES)],
        out_shape=[per_head, per_head, per_head,
                   jax.ShapeDtypeStruct((bsz, lp, D_MODEL), F32),
                   jax.ShapeDtypeStruct((bsz, lp, D_MODEL), BF16),
                   jax.ShapeDtypeStruct((bsz, lp, LANES), F32)],
        scratch_shapes=[pltpu.VMEM((3 * HEADS, SUBLANES + tb, LANES), F32),
                        pltpu.VMEM((HEADS, SUBLANES + tb, LANES), F32),
                        pltpu.VMEM((HEADS, tb, LANES), F32)],
        compiler_params=pltpu.CompilerParams(
            dimension_semantics=("arbitrary", "arbitrary"), vmem_limit_bytes=VMEM_LIMIT),
        name="inproj_first" if first else "inproj",
    )(*xs, ng, *weights, cw, scw, alog, dtb)


def _pair_masks():
    ri = lax.broadcasted_iota(jnp.int32, (CHUNK, LANES), 0)
    li = lax.broadcasted_iota(jnp.int32, (CHUNK, LANES), 1)
    left = li < CHUNK
    lj = jnp.where(left, li, li - CHUNK)
    return left, ri >= lj, ri > lj


def _pair_cols(left, x, a, b):
    return jnp.where(left, x[:, a:a + 1], x[:, b:b + 1])


def _decay_terms(gb_ref, nchunk):
    left, causal, _ = _pair_masks()
    r64 = lax.broadcasted_iota(jnp.int32, (CHUNK, CHUNK), 0)
    c64 = lax.broadcasted_iota(jnp.int32, (CHUNK, CHUNK), 1)
    ltri = (r64 >= c64).astype(BF16)
    e_g, e_kd, gam, dec = [], [], [], []
    for c in range(nchunk):
        g = gb_ref[c * CHUNK:(c + 1) * CHUNK, :]
        hi = g.astype(BF16)
        rest = g - hi.astype(F32)
        mid = rest.astype(BF16)
        lo = (rest - mid.astype(F32)).astype(BF16)
        gcum = (jnp.dot(ltri, hi, preferred_element_type=F32)
                + jnp.dot(ltri, mid, preferred_element_type=F32)
                + jnp.dot(ltri, lo, preferred_element_type=F32))
        grow = jnp.concatenate([gcum, gcum], axis=0).T[0:HEADS, :]
        e_g.append(jnp.exp(gcum))
        e_kd.append(jnp.exp(gcum[CHUNK - 1:CHUNK, :] - gcum))
        gam.append(jnp.broadcast_to(jnp.exp(grow[:, CHUNK - 1:CHUNK]), (HEADS, LANES)))
        for p in range(HEADS // 2):
            diff = (_pair_cols(left, gcum, 2 * p, 2 * p + 1)
                    - jnp.where(left, grow[2 * p:2 * p + 1, :], grow[2 * p + 1:2 * p + 2, :]))
            dec.append(jnp.where(causal, jnp.exp(diff), 0.0))
    return e_g, e_kd, gam, dec


def _chunk_terms(q_ref, k_ref, v_ref, gb_ref, decay, nchunk, hooks=()):
    hooks = list(hooks)

    def stage_done():
        if hooks:
            for fn in hooks.pop(0):
                fn()

    e_g, e_kd, _, dec = decay
    left, _, strict = _pair_masks()
    zero_k = jnp.zeros((CHUNK, HEAD_DIM), BF16)

    def block_diag(p):
        return jnp.concatenate([jnp.where(left, p, 0.0), jnp.where(left, 0.0, p)],
                               axis=0).astype(BF16)

    rows = [slice(c * CHUNK, (c + 1) * CHUNK) for c in range(nchunk)]
    gbc = [gb_ref[rows[c], :] for c in range(nchunk)]

    probs = [(c, h) for c in range(nchunk) for h in range(HEADS)]
    pairs = [(c, p) for c in range(nchunk) for p in range(HEADS // 2)]
    ps = range(len(probs))
    qs = range(len(pairs))
    qh = [q_ref[h, rows[c], :] for c, h in probs]
    kh = [k_ref[h, rows[c], :] for c, h in probs]
    bcol = [gbc[c][:, HEADS + h:HEADS + h + 1] for c, h in probs]
    egc = [e_g[c][:, h:h + 1] for c, h in probs]
    beta2 = [_pair_cols(left, gbc[c], HEADS + 2 * p, HEADS + 2 * p + 1) for c, p in pairs]
    qkk = []
    for i in qs:
        a, b = 2 * i, 2 * i + 1
        lhs = jnp.concatenate([jnp.concatenate([qh[a], qh[b]], axis=1),
                               jnp.concatenate([kh[a], kh[b]], axis=1)], axis=0).astype(BF16)
        ka, kb = kh[a].astype(BF16), kh[b].astype(BF16)
        rhs_t = jnp.concatenate([jnp.concatenate([ka, zero_k], axis=1),
                                 jnp.concatenate([zero_k, kb], axis=1)], axis=0)
        qkk.append(lax.dot_general(lhs, rhs_t, (((1,), (1,)), ((), ())),
                                   preferred_element_type=F32))
    stage_done()
    xm = [jnp.where(strict, -(beta2[i] * qkk[i][CHUNK:] * dec[i]), 0.0) for i in qs]
    pw = [jnp.dot(xm[i].astype(BF16), block_diag(xm[i]), preferred_element_type=F32) for i in qs]
    stage_done()
    for _ in range(4):
        both = [jnp.dot(jnp.concatenate([xm[i], pw[i]], axis=0).astype(BF16), block_diag(pw[i]),
                        preferred_element_type=F32) for i in qs]
        stage_done()
        xm = [xm[i] + pw[i] + both[i][:CHUNK] for i in qs]
        pw = [both[i][CHUNK:] for i in qs]
    last = [jnp.dot(xm[i].astype(BF16), block_diag(pw[i]), preferred_element_type=F32) for i in qs]
    stage_done()
    xmb = [(xm[i] + pw[i] + last[i]).astype(BF16) for i in qs]
    pmb = [(qkk[i][:CHUNK] * dec[i]).astype(BF16) for i in qs]

    def pad_rows(m, p):
        z = jnp.zeros_like(m)
        return jnp.concatenate([m, z] if p % 2 == 0 else [z, m], axis=0)

    rhs = [jnp.concatenate([bcol[p] * v_ref[h, rows[c], :], (bcol[p] * egc[p]) * kh[p]], axis=1)
           for p, (c, h) in enumerate(probs)]
    xr = [jnp.dot(xmb[p // 2], pad_rows(rhs[p].astype(BF16), p), preferred_element_type=F32)
          for p in ps]
    stage_done()
    uw = [(rhs[p] + xr[p]).astype(BF16) for p in ps]
    cg = [lax.dot_general((kh[p] * e_kd[c][:, h:h + 1]).astype(BF16), uw[p],
                          (((0,), (0,)), ((), ())), preferred_element_type=F32)
          for p, (c, h) in enumerate(probs)]
    stage_done()
    op = [jnp.dot(pmb[p // 2], pad_rows(uw[p], p), preferred_element_type=F32) for p in ps]
    stage_done()
    while hooks:
        stage_done()
    cm = [cg[p][:, :HEAD_DIM] for p in ps]
    gm = [cg[p][:, HEAD_DIM:].astype(BF16) for p in ps]
    oin = [op[p][:, :HEAD_DIM] for p in ps]
    qt = [(qh[p] * egc[p] - op[p][:, HEAD_DIM:]).astype(BF16) for p in ps]
    return cm, gm, qt, oin


def _delta_body(*refs, first, final, nchunk, nblk, row_block):
    if first:
        xprev_ref, xcur_ref, meta_ref = refs[:3]
        refs = refs[3:]
    else:
        x_ref = refs[0]
        refs = refs[1:]
    (q_ref, k_ref, v_ref, gb_ref, sz_ref, osc_ref, og_ref, wout_ref, fg_ref,
     y_ref, s_scr, cm_scr, gm_scr, qt_scr, oin_scr, gam_scr) = refs[:16]
    carry_scr = refs[16] if final else None
    t = pl.program_id(0)
    cur = lax.rem(t, 2)
    prev = 1 - cur
    jprev = lax.rem(jnp.maximum(t - 1, 0), nblk)

    @pl.when(t == 0)
    def _():
        s_scr[...] = jnp.zeros(s_scr.shape, F32)
        cm_scr[1] = jnp.zeros(cm_scr.shape[1:], F32)
        gm_scr[1] = jnp.zeros(gm_scr.shape[1:], BF16)
        qt_scr[1] = jnp.zeros(qt_scr.shape[1:], BF16)
        oin_scr[1] = jnp.zeros(oin_scr.shape[1:], F32)
        gam_scr[1] = jnp.zeros(gam_scr.shape[1:], F32)
        if final:
            carry_scr[...] = jnp.zeros(carry_scr.shape, F32)

    hs = range(HEADS)
    sls = [slice(h * HEAD_DIM, (h + 1) * HEAD_DIM) for h in hs]
    first_block = jprev == 0
    s = [jnp.where(first_block, 0.0, s_scr[h]) for h in hs]
    o = [[] for _ in hs]
    mix = []
    pieces = []
    wide = 2 * LANES

    def scan_step(c):
        def run():
            ps = [c * HEADS + h for h in hs]
            gam = gam_scr[prev, c]
            r = [jnp.dot(jnp.concatenate([gm_scr[prev, ps[h]], qt_scr[prev, ps[h]]], axis=0),
                         s[h].astype(BF16), preferred_element_type=F32) for h in hs]
            for h in hs:
                o[h].append(oin_scr[prev, ps[h]] + r[h][HEAD_DIM:])
                s[h] = gam[h:h + 1, :] * s[h] + cm_scr[prev, ps[h]] - r[h][:HEAD_DIM]
        return run

    def gate():
        for h in hs:
            s_scr[h] = s[h]
        parts = []
        for h in hs:
            oh = jnp.concatenate(o[h], axis=0)
            on = oh * lax.rsqrt(jnp.mean(oh * oh, axis=-1, keepdims=True) + EPS) * og_ref[...]
            parts.append((on * sz_ref[:, sls[h]]).astype(BF16))
        parts.append(osc_ref[...])
        mix.append(jnp.concatenate(parts, axis=1))

    def out_piece(n):
        def run():
            cols = slice(n * wide, (n + 1) * wide)
            if first:
                res = _first_layer_rows(xprev_ref, xcur_ref, meta_ref, jprev, row_block, cols)
            else:
                res = x_ref[:, cols]
            y = res + jnp.dot(mix[0], wout_ref[:, cols], preferred_element_type=F32)
            row = lax.broadcasted_iota(jnp.int32, y.shape, 0) + jprev * row_block
            y = jnp.where(row >= PAD_FRONT, y, 0.0)
            if final:
                pieces.append(y)
            else:
                y_ref[:, cols] = y
        return run

    steps = [scan_step(c) for c in range(nchunk)]
    outs = [out_piece(n) for n in range(D_MODEL // wide)]
    hooks = [steps[:4], steps[4:], [gate]] + [[fn] for fn in outs]
    decay = _decay_terms(gb_ref, nchunk)
    cm, gm, qt, oin = _chunk_terms(q_ref, k_ref, v_ref, gb_ref, decay, nchunk, hooks)
    if final:
        front = PAD_FRONT + N_META
        y = jnp.concatenate(pieces, axis=1)
        y = y * lax.rsqrt(jnp.mean(y * y, axis=-1, keepdims=True) + EPS) * fg_ref[...]
        y_ref[0:row_block - front, :] = carry_scr[...]
        y_ref[row_block - front:row_block, :] = y[0:front]
        carry_scr[...] = y[front:row_block]

    for p in range(nchunk * HEADS):
        cm_scr[cur, p] = cm[p]
        gm_scr[cur, p] = gm[p]
        qt_scr[cur, p] = qt[p]
        oin_scr[cur, p] = oin[p]
    for c in range(nchunk):
        gam_scr[cur, c] = decay[2][c]


def _delta(xs, q, k, v, gb, sz, osc, layer, og, wout, fg, final):
    first = len(xs) == 3
    bsz, _, lp, _ = q.shape
    tb = ROW_BLOCK
    nchunk = tb // CHUNK
    nblk = lp // tb
    total = bsz * nblk
    front = PAD_FRONT + N_META

    def cur_idx(t):
        tc = jnp.minimum(t, total - 1)
        return tc // nblk, tc % nblk

    def prev_idx(t):
        tp = jnp.clip(t - 1, 0, total - 1)
        return tp // nblk, tp % nblk

    def cur_heads(t):
        b, j = cur_idx(t)
        return b, 0, j, 0

    def cur_rows(t):
        b, j = cur_idx(t)
        return b, j, 0

    def prev_rows(t):
        b, j = prev_idx(t)
        return b, j, 0

    def prev_rows_before(t):
        b, j = prev_idx(t)
        return b, jnp.maximum(j - 1, 0), 0

    def final_rows(t):
        tp = jnp.clip(t - 2, 0, total - 1)
        return tp // nblk, tp % nblk, 0

    heads = pl.BlockSpec((None, HEADS, tb, HEAD_DIM), cur_heads)
    prow = pl.BlockSpec((None, tb, D_MODEL), prev_rows)
    if first:
        x_specs = [pl.BlockSpec((None, tb, D_MODEL), prev_rows_before), prow,
                   _const_spec(xs[2].shape)]
    else:
        x_specs = [prow]
    nprob = nchunk * HEADS
    return pl.pallas_call(
        functools.partial(_delta_body, first=first, final=final, nchunk=nchunk, nblk=nblk,
                          row_block=tb),
        grid=(total + 2 if final else total + 1,),
        in_specs=x_specs + [heads, heads, heads, pl.BlockSpec((None, tb, LANES), cur_rows), prow, prow,
                  _const_spec(og.shape), _layer_spec(layer, 2 * D_MODEL, D_MODEL),
                  _const_spec(fg.shape)],
        out_specs=pl.BlockSpec((None, tb, D_MODEL), final_rows) if final else prow,
        out_shape=jax.ShapeDtypeStruct((bsz, lp - front if final else lp, D_MODEL), F32),
        scratch_shapes=[pltpu.VMEM((HEADS, HEAD_DIM, HEAD_DIM), F32),
                        pltpu.VMEM((2, nprob, HEAD_DIM, HEAD_DIM), F32),
                        pltpu.VMEM((2, nprob, HEAD_DIM, HEAD_DIM), BF16),
                        pltpu.VMEM((2, nprob, CHUNK, HEAD_DIM), BF16),
                        pltpu.VMEM((2, nprob, CHUNK, HEAD_DIM), F32),
                        pltpu.VMEM((2, nchunk, HEADS, LANES), F32)]
        + ([pltpu.VMEM((tb - front, D_MODEL), F32)] if final else []),
        compiler_params=pltpu.CompilerParams(
            dimension_semantics=("arbitrary",), vmem_limit_bytes=VMEM_LIMIT),
        name="delta_final" if final else ("delta_first" if first else "delta"),
    )(*xs, q, k, v, gb, sz, osc, og, wout, fg)


def _pad_lanes(v, fill=0.0):
    return jnp.pad(v.astype(F32), (0, LANES - v.shape[0]), constant_values=fill)[None, :]


def kernel(x, meta_tokens, norm_g, w_in, dn_conv_w, dn_A_log, dn_dt_bias, dn_out_g, sc_conv_w,
           w_out, final_g):
    bsz, seq, d = x.shape
    depth = w_in.shape[0]
    dn = HEADS * HEAD_DIM
    sc = (w_in.shape[2] - 4 * dn - 2 * HEADS) // 4
    assert d == D_MODEL and sc == D_MODEL and dn == D_MODEL
    lp = PAD_FRONT + N_META + seq
    assert lp % ROW_BLOCK == 0 and lp % INPROJ_ROWS == 0

    xs = (x, x, meta_tokens.astype(x.dtype))

    w_all = w_in.astype(BF16)
    w_sc = w_all[:, :, 4 * dn + 2 * HEADS:]
    wout = w_out.astype(BF16)
    for l in range(depth):
        q, k, v, sz, osc, gb = _inproj(
            xs, lp, l, norm_g[l][None, :], w_all, w_sc,
            dn_conv_w[l], sc_conv_w[l], _pad_lanes(dn_A_log[l]), _pad_lanes(dn_dt_bias[l]))
        xp = _delta(xs, q, k, v, gb, sz, osc, l, dn_out_g[l][None, :], wout,
                    final_g[None, :], final=(l == depth - 1))
        xs = (xp,)
    return xp
```
